```python
import math
import jax, jax.numpy as jnp
from jax import lax
import numpy as np

D_MODEL = 2048
BATCH = 4
SEQ = 2048
DEPTH = 4
DEC_BATCH = 128
DEC_SEQ = 8
PAST_LEN = 16384
PAGE_SIZE = 128

N_EVEN = (DEPTH + 1) // 2
N_ODD = DEPTH // 2
D_SSD = D_MODEL
SSD_HEAD_DIM = 64
SSD_HEADS = D_SSD // SSD_HEAD_DIM
SSD_GROUPS = 8
SSD_STATE = 128
SSD_CONV = 4
SSD_CHUNK = 128
SSD_CONV_CH = D_SSD + 2 * SSD_GROUPS * SSD_STATE
D_SC = D_MODEL
SC_GROUPS = 16
SC_WIDTH = 3
POOL_WINDOWS = (2, 4, 8, 16)
POOL_GROUPS = 4
POOL_GROUP_DIM = D_MODEL // POOL_GROUPS
POOL_CTX = 15
MEM_LEN = 256
X_HEADS = 4
X_HEAD_DIM = D_MODEL // X_HEADS
D_FF = 4 * D_MODEL
D_IN_EVEN = D_SSD + SSD_CONV_CH + SSD_HEADS + 3 * D_SC
ALPHA = (2.0 * DEPTH) ** 0.25
BETA = (8.0 * DEPTH) ** -0.25
LN_EPS = 1e-5
RMS_EPS = 1e-5

kernel_name = "hybrid_ssd_shortconv_pool_memxattn_step"


def layer_norm(x, g, b):
    xf = x.astype(jnp.float32)
    mu = jnp.mean(xf, -1, keepdims=True)
    var = jnp.mean(jnp.square(xf - mu), -1, keepdims=True)
    return ((xf - mu) * lax.rsqrt(var + LN_EPS) * g + b).astype(x.dtype)


def causal_dwconv(u, ctx, w):
    K = w.shape[0]
    T = u.shape[1]
    full = jnp.concatenate([ctx.astype(u.dtype), u], axis=1)
    out = full[:, 0:T] * w[0]
    for k in range(1, K):
        out = out + full[:, k:k + T] * w[k]
    return out, full[:, T:]


def ssd_scan(xh, dt, a_head, bmat, cmat, h0):
    f32 = jnp.float32
    b, T, H, P = xh.shape
    G, N = bmat.shape[2], bmat.shape[3]
    R = H // G
    L = SSD_CHUNK if T % SSD_CHUNK == 0 else T
    nc = T // L
    xdt = (xh.astype(f32) * dt[..., None]).reshape(b, nc, L, G, R, P)
    a = (dt * a_head).reshape(b, nc, L, G, R)
    acs = jnp.moveaxis(jnp.cumsum(a, axis=2), 2, -1)
    Bc = bmat.astype(f32).reshape(b, nc, L, G, N)
    Cc = cmat.astype(f32).reshape(b, nc, L, G, N)
    causal = jnp.tril(jnp.ones((L, L), dtype=bool))
    decay = jnp.exp(jnp.where(causal, acs[..., :, None] - acs[..., None, :], -jnp.inf))
    cb = jnp.einsum('bclgn,bcsgn->bcgls', Cc, Bc)
    y_diag = jnp.einsum('bcgls,bcgrls,bcsgrp->bclgrp', cb, decay, xdt)
    decay_end = jnp.exp(acs[..., -1:] - acs)
    states = jnp.einsum('bclgn,bcgrl,bclgrp->bcgrpn', Bc, decay_end, xdt)
    chunk_decay = jnp.exp(acs[..., -1])

    def step(h, inp):
        s, d = inp
        return h * d[..., None, None] + s, h

    h_init = h0.astype(f32).reshape(b, G, R, P, N)
    h_last, h_prev = lax.scan(step, h_init, (jnp.moveaxis(states, 1, 0), jnp.moveaxis(chunk_decay, 1, 0)))
    h_prev = jnp.moveaxis(h_prev, 0, 1)
    y_off = jnp.einsum('bclgn,bcgrpn,bcgrl->bclgrp', Cc, h_prev, jnp.exp(acs))
    y = (y_diag + y_off).reshape(b, T, H, P)
    return y, h_last.reshape(b, H, P, N)


def even_mixer(x, st_ssd, st_ssd_conv, st_sc, w_in, conv_w, conv_b, dt_bias, a_log, d_skip, norm_g, sc_w, w_out):
    f32 = jnp.float32
    b, T, _ = x.shape
    proj = x @ w_in
    o1 = D_SSD
    o2 = o1 + SSD_CONV_CH
    o3 = o2 + SSD_HEADS
    o4 = o3 + D_SC
    o5 = o4 + D_SC
    z, xbc, dt_raw, g_b, g_c, h_sc = jnp.split(proj, [o1, o2, o3, o4, o5], axis=-1)
    xbc, new_ssd_conv = causal_dwconv(xbc, st_ssd_conv, conv_w)
    xbc = jax.nn.silu(xbc + conv_b)
    xs, bm, cm = jnp.split(xbc, [D_SSD, D_SSD + SSD_GROUPS * SSD_STATE], axis=-1)
    xs = xs.reshape(b, T, SSD_HEADS, SSD_HEAD_DIM)
    bm = bm.reshape(b, T, SSD_GROUPS, SSD_STATE)
    cm = cm.reshape(b, T, SSD_GROUPS, SSD_STATE)
    dt = jax.nn.softplus(dt_raw.astype(f32) + dt_bias.astype(f32))
    a_head = -jnp.exp(a_log.astype(f32))
    y, new_h = ssd_scan(xs, dt, a_head, bm, cm, st_ssd)
    y = y + d_skip.astype(f32)[:, None] * xs.astype(f32)
    y = y.reshape(b, T, D_SSD) * jax.nn.silu(z.astype(f32))
    yg = y.reshape(b, T, SSD_GROUPS, D_SSD // SSD_GROUPS)
    yg = yg * lax.rsqrt(jnp.mean(yg * yg, -1, keepdims=True) + RMS_EPS)
    y_ssd = (yg.reshape(b, T, D_SSD) * norm_g.astype(f32)).astype(x.dtype)
    v, new_sc = causal_dwconv(g_c * h_sc, st_sc, sc_w)
    y_sc = g_b * v
    out = jnp.concatenate([y_ssd, y_sc], axis=-1) @ w_out
    return out, new_h.astype(st_ssd.dtype), new_ssd_conv, new_sc


def pool_mixer(x, st_pool, start_pos, w_pool, pool_scale):
    f32 = jnp.float32
    b, T, _ = x.shape
    full = jnp.concatenate([st_pool.astype(x.dtype), x], axis=1)
    cs = jnp.cumsum(full.astype(f32), axis=1)
    cs = jnp.concatenate([jnp.zeros((b, 1, D_MODEL), f32), cs], axis=1)
    pos = start_pos + jnp.arange(T)
    xf = x.astype(f32)
    outs = []
    for gi, w in enumerate(POOL_WINDOWS):
        lo, hi = gi * POOL_GROUP_DIM, (gi + 1) * POOL_GROUP_DIM
        s = cs[:, POOL_CTX + 1:POOL_CTX + 1 + T, lo:hi] - cs[:, POOL_CTX + 1 - w:POOL_CTX + 1 - w + T, lo:hi]
        cnt = jnp.minimum(pos + 1, w).astype(f32)[None, :, None]
        outs.append(s / cnt - xf[..., lo:hi])
    p = jnp.stack(outs, axis=2).astype(x.dtype)
    y = jnp.einsum('btgi,gio->btgo', p, w_pool).reshape(b, T, D_MODEL) * pool_scale
    return y, full[:, T:]


def cross_attn(x, k, v, wq, wo):
    b, T, _ = x.shape
    q = (x @ wq).reshape(b, T, X_HEADS, X_HEAD_DIM)
    s = jnp.einsum('bthd,bmhd->bhtm', q, k).astype(jnp.float32) * (X_HEAD_DIM ** -0.5)
    p = jax.nn.softmax(s, axis=-1).astype(x.dtype)
    o = jnp.einsum('bhtm,bmhd->bthd', p, v).reshape(b, T, D_MODEL)
    return o @ wo


def mlp(x, w_up, w_down):
    return jnp.square(jax.nn.relu(x @ w_up)) @ w_down


def trunk(x, start_pos, ssd_h, ssd_cb, sc_cb, pool_cb, mem_k, mem_v,
          w_in_even, ssd_conv_w, ssd_conv_b, ssd_dt_bias, ssd_a_log, ssd_d, ssd_norm_g, sc_conv_w,
          w_out_even, w_pool, pool_scale, wq_x, wo_x, w_up, w_down, ln_g, ln_b):
    hs, cbs, sbs, pbs = [], [], [], []
    for layer in range(DEPTH):
        if layer % 2 == 0:
            e = layer // 2
            m, h, cb, sb = even_mixer(x, ssd_h[e], ssd_cb[e], sc_cb[e], w_in_even[e], ssd_conv_w[e],
                                      ssd_conv_b[e], ssd_dt_bias[e], ssd_a_log[e], ssd_d[e],
                                      ssd_norm_g[e], sc_conv_w[e], w_out_even[e])
            hs.append(h)
            cbs.append(cb)
            sbs.append(sb)
        else:
            o = layer // 2
            m, pb = pool_mixer(x, pool_cb[o], start_pos, w_pool[o], pool_scale[o])
            pbs.append(pb)
        x = layer_norm(ALPHA * x + m, ln_g[layer, 0], ln_b[layer, 0])
        x = layer_norm(ALPHA * x + cross_attn(x, mem_k[layer], mem_v[layer], wq_x[layer], wo_x[layer]),
                       ln_g[layer, 1], ln_b[layer, 1])
        x = layer_norm(ALPHA * x + mlp(x, w_up[layer], w_down[layer]), ln_g[layer, 2], ln_b[layer, 2])
    return x, jnp.stack(hs), jnp.stack(cbs), jnp.stack(sbs), jnp.stack(pbs)


def setup_inputs(seed: int = 0) -> dict:
    key = jax.random.key(seed)
    ks = jax.random.split(key, 32)
    nrm = jax.random.normal
    D = D_MODEL
    dt0 = jnp.exp(jax.random.uniform(ks[12], (N_EVEN, SSD_HEADS)) * (math.log(0.1) - math.log(0.001)) + math.log(0.001))
    return {
        "x_prompt": nrm(ks[0], (BATCH, SEQ, D), jnp.float32),
        "x_sample": nrm(ks[1], (DEC_BATCH, DEC_SEQ, D), jnp.float32),
        "state_ssd": 0.2 * nrm(ks[2], (N_EVEN, DEC_BATCH, SSD_HEADS, SSD_HEAD_DIM, SSD_STATE), jnp.float32),
        "state_ssd_conv": nrm(ks[3], (N_EVEN, DEC_BATCH, SSD_CONV - 1, SSD_CONV_CH), jnp.float32),
        "state_short_conv": nrm(ks[4], (N_EVEN, DEC_BATCH, SC_WIDTH - 1, D_SC), jnp.float32),
        "state_pool": nrm(ks[5], (N_ODD, DEC_BATCH, POOL_CTX, D), jnp.float32),
        "cache_mem_k": nrm(ks[6], (DEPTH, DEC_BATCH, MEM_LEN, X_HEADS, X_HEAD_DIM), jnp.float32),
        "cache_mem_v": BETA * nrm(ks[7], (DEPTH, DEC_BATCH, MEM_LEN, X_HEADS, X_HEAD_DIM), jnp.float32),
        "mem_prompt": nrm(ks[8], (BATCH, MEM_LEN, D), jnp.float32),
        "w_in_even": nrm(ks[9], (N_EVEN, D, D_IN_EVEN), jnp.float32) * D ** -0.5,
        "ssd_conv_w": nrm(ks[10], (N_EVEN, SSD_CONV, SSD_CONV_CH), jnp.float32) * SSD_CONV ** -0.5,
        "ssd_conv_b": 0.01 * nrm(ks[11], (N_EVEN, SSD_CONV_CH), jnp.float32),
        "ssd_dt_bias": dt0 + jnp.log(-jnp.expm1(-dt0)),
        "ssd_a_log": jnp.log(jax.random.uniform(ks[13], (N_EVEN, SSD_HEADS), minval=1.0, maxval=16.0)),
        "ssd_d": 1.0 + 0.02 * nrm(ks[14], (N_EVEN, SSD_HEADS), jnp.float32),
        "ssd_norm_g": 1.0 + 0.02 * nrm(ks[15], (N_EVEN, D_SSD), jnp.float32),
        "sc_conv_w": nrm(ks[16], (N_EVEN, SC_WIDTH, D_SC), jnp.float32) * SC_WIDTH ** -0.5,
        "w_out_even": nrm(ks[17], (N_EVEN, D_SSD + D_SC, D), jnp.float32) * (D_SSD + D_SC) ** -0.5 * BETA,
        "w_pool": nrm(ks[18], (N_ODD, POOL_GROUPS, POOL_GROUP_DIM, POOL_GROUP_DIM), jnp.float32) * POOL_GROUP_DIM ** -0.5 * BETA,
        "pool_scale": 1.0 + 0.02 * nrm(ks[19], (N_ODD, D), jnp.float32),
        "wq_x": nrm(ks[20], (DEPTH, D, D), jnp.float32) * D ** -0.5,
        "wk_x": nrm(ks[21], (DEPTH, D, D), jnp.float32) * D ** -0.5,
        "wv_x": nrm(ks[22], (DEPTH, D, D), jnp.float32) * D ** -0.5 * BETA,
        "wo_x": nrm(ks[23], (DEPTH, D, D), jnp.float32) * D ** -0.5 * BETA,
        "w_up": nrm(ks[24], (DEPTH, D, D_FF), jnp.float32) * D ** -0.5 * BETA,
        "w_down": nrm(ks[25], (DEPTH, D_FF, D), jnp.float32) * D_FF ** -0.5 * BETA,
        "ln_g": 1.0 + 0.02 * nrm(ks[26], (DEPTH, 3, D), jnp.float32),
        "ln_b": 0.02 * nrm(ks[27], (DEPTH, 3, D), jnp.float32),
    }


def reference(x_prompt, x_sample, state_ssd, state_ssd_conv, state_short_conv, state_pool,
              cache_mem_k, cache_mem_v, mem_prompt,
              w_in_even, ssd_conv_w, ssd_conv_b, ssd_dt_bias, ssd_a_log, ssd_d, ssd_norm_g, sc_conv_w,
              w_out_even, w_pool, pool_scale, wq_x, wk_x, wv_x, wo_x, w_up, w_down, ln_g, ln_b):
    weights = (w_in_even, ssd_conv_w, ssd_conv_b, ssd_dt_bias, ssd_a_log, ssd_d, ssd_norm_g, sc_conv_w,
               w_out_even, w_pool, pool_scale, wq_x, wo_x, w_up, w_down, ln_g, ln_b)
    bp = x_prompt.shape[0]
    dtp = x_prompt.dtype
    h0 = jnp.zeros((N_EVEN, bp, SSD_HEADS, SSD_HEAD_DIM, SSD_STATE), dtp)
    cb0 = jnp.zeros((N_EVEN, bp, SSD_CONV - 1, SSD_CONV_CH), dtp)
    sb0 = jnp.zeros((N_EVEN, bp, SC_WIDTH - 1, D_SC), dtp)
    pb0 = jnp.zeros((N_ODD, bp, POOL_CTX, D_MODEL), dtp)
    mk_p = jnp.einsum('bmd,lde->lbme', mem_prompt, wk_x).reshape(DEPTH, bp, MEM_LEN, X_HEADS, X_HEAD_DIM)
    mv_p = jnp.einsum('bmd,lde->lbme', mem_prompt, wv_x).reshape(DEPTH, bp, MEM_LEN, X_HEADS, X_HEAD_DIM)
    y_prompt, h_p, cb_p, sb_p, pb_p = trunk(x_prompt, 0, h0, cb0, sb0, pb0, mk_p, mv_p, *weights)
    y_sample, h_s, cb_s, sb_s, pb_s = trunk(x_sample, PAST_LEN, state_ssd, state_ssd_conv, state_short_conv,
                                            state_pool, cache_mem_k, cache_mem_v, *weights)
    return (y_prompt, y_sample, h_p, h_s, cb_p, cb_s, sb_p, sb_s, pb_p, pb_s, mk_p, mv_p)
```

```python
import functools
import math

import jax
import jax.numpy as jnp
from jax import lax
from jax.experimental import pallas as pl
from jax.experimental.pallas import tpu as pltpu

F32 = jnp.float32
BF16 = jnp.bfloat16
HIGHEST = lax.Precision.HIGHEST

PAST_LEN = 16384
SSD_CHUNK = 128
POOL_WINDOWS = (2, 4, 8, 16)
POOL_CTX = 15
LN_EPS = 1e-5
RMS_EPS = 1e-5

V7X_LANES = 128
V7X_SUBLANES = 8
V7X_VMEM_BYTES = 64 * 1024 * 1024
VMEM_LIMIT = V7X_VMEM_BYTES - 8 * 1024 * 1024


def _tile(dim, pref, quantum):
    best = None
    t = quantum
    while t <= min(dim, pref):
        if dim % t == 0:
            best = t
        t += quantum
    return best if best is not None else dim


def _params(*sem):
    return pltpu.CompilerParams(dimension_semantics=sem, vmem_limit_bytes=VMEM_LIMIT)


def _silu(x):
    return x * jax.nn.sigmoid(x)


def _softplus(x):
    return jnp.maximum(x, 0.0) + jnp.log1p(jnp.exp(-jnp.abs(x)))


def _layer_norm(y, g, b):
    mu = jnp.mean(y, -1, keepdims=True)
    d = y - mu
    var = jnp.mean(d * d, -1, keepdims=True)
    return d * lax.rsqrt(var + LN_EPS) * g + b


def _mm_kernel(x_ref, w_ref, o_ref, *, act):
    acc = jnp.dot(x_ref[...], w_ref[...], preferred_element_type=F32)
    if act == "relu2":
        acc = jnp.square(jnp.maximum(acc, 0.0))
    o_ref[...] = acc.astype(o_ref.dtype)


def matmul(x, w, out_dtype, act=None, bm=1024, bn=1024):
    M, K = x.shape
    N = w.shape[1]
    bm = _tile(M, bm, V7X_SUBLANES)
    bn = _tile(N, bn, V7X_LANES)
    return pl.pallas_call(
        functools.partial(_mm_kernel, act=act),
        grid=(N // bn, M // bm),
        in_specs=[pl.BlockSpec((bm, K), lambda j, i: (i, 0)),
                  pl.BlockSpec((K, bn), lambda j, i: (0, j))],
        out_specs=pl.BlockSpec((bm, bn), lambda j, i: (i, j)),
        out_shape=jax.ShapeDtypeStruct((M, N), out_dtype),
        compiler_params=_params("parallel", "parallel"),
    )(x, w)


def _mm_ln_kernel(x_ref, w_ref, res_ref, g_ref, b_ref, of_ref, ob_ref, *scratch, nk, alpha):
    part = jnp.dot(x_ref[...], w_ref[...], preferred_element_type=F32)

    def finish(m):
        out = _layer_norm(alpha * res_ref[...] + m, g_ref[...], b_ref[...])
        of_ref[...] = out
        ob_ref[...] = out.astype(BF16)

    if nk == 1:
        finish(part)
        return
    acc_ref, = scratch
    k = pl.program_id(1)

    @pl.when(k == 0)
    def _():
        acc_ref[...] = part

    @pl.when(jnp.logical_and(k > 0, k < nk - 1))
    def _():
        acc_ref[...] += part

    @pl.when(k == nk - 1)
    def _():
        finish(acc_ref[...] + part)


def matmul_ln(x, w, res, g, b, alpha, bm=512, bk=1024):
    M, K = x.shape
    N = w.shape[1]
    bm = _tile(M, bm, V7X_SUBLANES)
    bk = K if K <= 2048 else _tile(K, bk, V7X_LANES)
    nk = K // bk
    scratch = [] if nk == 1 else [pltpu.VMEM((bm, N), F32)]
    return pl.pallas_call(
        functools.partial(_mm_ln_kernel, nk=nk, alpha=alpha),
        grid=(M // bm, nk),
        in_specs=[pl.BlockSpec((bm, bk), lambda i, k: (i, k)),
                  pl.BlockSpec((bk, N), lambda i, k: (k, 0)),
                  pl.BlockSpec((bm, N), lambda i, k: (i, 0)),
                  pl.BlockSpec((1, N), lambda i, k: (0, 0)),
                  pl.BlockSpec((1, N), lambda i, k: (0, 0))],
        out_specs=[pl.BlockSpec((bm, N), lambda i, k: (i, 0)),
                   pl.BlockSpec((bm, N), lambda i, k: (i, 0))],
        out_shape=[jax.ShapeDtypeStruct((M, N), F32), jax.ShapeDtypeStruct((M, N), BF16)],
        scratch_shapes=scratch,
        compiler_params=_params("parallel", "arbitrary"),
    )(x, w, res, g, b)


def _even_mixer_kernel(*refs, L, nc, G, R, P, N, has_init):
    (z_ref, xs_ref, bc_ref, gb_ref, gc_ref, hh_ref, dt_ref, convw_ref, convb_ref, dtb_ref, alog_ref,
     dskip_ref, normg_ref, scw_ref, tril_ref, expand_ref) = refs[:16]
    pos = 16
    if has_init:
        h0_ref, ctx_ref, scctx_ref = refs[pos:pos + 3]
        pos += 3
    y_ref, hout_ref, utail_ref = refs[pos:pos + 3]
    state_ref, ext_ref, act_ref, extu_ref = refs[pos + 3:]

    RP = R * P
    D = G * RP
    GN = G * N
    CH = D + 2 * GN
    KC = convw_ref.shape[0]
    KS = scw_ref.shape[0]
    c = pl.program_id(1)

    @pl.when(c == 0)
    def _init():
        if has_init:
            ext_ref[8 - (KC - 1):8, :] = ctx_ref[0]
            extu_ref[8 - (KS - 1):8, :] = scctx_ref[0]
            for g in range(G):
                state_ref[g] = h0_ref[0, g * R:(g + 1) * R].reshape(RP, N)
        else:
            ext_ref[0:8, :] = jnp.zeros((8, CH), F32)
            extu_ref[0:8, :] = jnp.zeros((8, D), F32)
            state_ref[...] = jnp.zeros(state_ref.shape, F32)

    ext_ref[8:8 + L, 0:D] = xs_ref[...]
    ext_ref[8:8 + L, D:CH] = bc_ref[...]
    cw = min(512, CH)
    for j in range(CH // cw):
        cols = slice(j * cw, (j + 1) * cw)
        base = 8 - (KC - 1)
        acc = ext_ref[base:base + L, cols] * convw_ref[0:1, cols]
        for k in range(1, KC):
            acc = acc + ext_ref[base + k:base + k + L, cols] * convw_ref[k:k + 1, cols]
        act_ref[:, cols] = _silu(acc + convb_ref[:, cols])

    dt = _softplus(dt_ref[...] + dtb_ref[...])
    a = dt * (-jnp.exp(alog_ref[...]))
    acs = jnp.dot(tril_ref[...], a, precision=HIGHEST, preferred_element_type=F32)
    dt_e = jnp.dot(dt, expand_ref[...], precision=HIGHEST, preferred_element_type=F32)
    acs_e = jnp.dot(acs, expand_ref[...], precision=HIGHEST, preferred_element_type=F32)
    if L == V7X_LANES:
        acs_row = acs.T
    else:
        acs_row = jnp.concatenate([acs, jnp.zeros((V7X_LANES - L, V7X_LANES), F32)], axis=0).T[:, 0:L]

    li = lax.broadcasted_iota(jnp.int32, (L, L), 0)
    si = lax.broadcasted_iota(jnp.int32, (L, L), 1)
    causal = si <= li
    ci = lax.broadcasted_iota(jnp.int32, (L, RP), 1)
    head_cols = [jnp.logical_and(ci >= r * P, ci < (r + 1) * P) for r in range(R)]

    for g in range(G):
        gc_ = slice(g * RP, (g + 1) * RP)
        xs_g = act_ref[:, gc_]
        b_g = act_ref[:, D + g * N:D + (g + 1) * N]
        c_g = act_ref[:, D + GN + g * N:D + GN + (g + 1) * N].astype(BF16)
        xdt = xs_g * dt_e[:, gc_]
        x_bd = jnp.concatenate([jnp.where(head_cols[r], xdt, 0.0) for r in range(R)], axis=0).astype(BF16)
        acs_g = acs_e[:, gc_]
        if has_init:
            cb = lax.dot_general(c_g, b_g.astype(BF16), (((1,), (1,)), ((), ())), preferred_element_type=F32)
        else:
            bt_g = b_g.T
            cb = jnp.dot(c_g, bt_g.astype(BF16), preferred_element_type=F32)

        ms, dbs = [], []
        for r in range(R):
            h = g * R + r
            col = acs[:, h:h + 1]
            row = acs_row[h:h + 1, :]
            decay = jnp.exp(jnp.where(causal, col - row, -jnp.inf))
            ms.append(cb * decay)
            if has_init:
                dbs.append(jnp.exp(acs[L - 1:L, h:h + 1] - col) * b_g)
            else:
                dbs.append(bt_g * jnp.exp(acs_row[h:h + 1, L - 1:L] - row))

        if L % V7X_LANES == 0:
            y = jnp.dot(jnp.concatenate(ms, axis=1).astype(BF16), x_bd, preferred_element_type=F32)
        else:
            y = jnp.zeros((L, RP), F32)
            for r in range(R):
                y = y + jnp.dot(ms[r].astype(BF16), x_bd[r * L:(r + 1) * L], preferred_element_type=F32)

        h_prev = state_ref[g]
        if has_init:
            y_off = lax.dot_general(c_g, h_prev.astype(BF16), (((1,), (1,)), ((), ())),
                                    preferred_element_type=F32)
            db = jnp.concatenate(dbs, axis=0).astype(BF16)
            new_states = lax.dot_general(x_bd, db, (((0,), (0,)), ((), ())), preferred_element_type=F32)
            cd = jnp.concatenate(
                [jnp.broadcast_to(jnp.exp(acs[L - 1:L, g * R + r:g * R + r + 1]), (P, N)) for r in range(R)], axis=0)
            state_ref[g] = h_prev * cd + new_states
        else:
            y_off = jnp.dot(c_g, h_prev.astype(BF16), preferred_element_type=F32)
            dbt = jnp.concatenate(dbs, axis=1).astype(BF16)
            new_states = jnp.dot(dbt, x_bd, preferred_element_type=F32)
            state_ref[g] = h_prev * jnp.exp(acs_g[L - 1:L, :]) + new_states
        y = y + y_off * jnp.exp(acs_g)
        y = y + dskip_ref[:, gc_] * xs_g
        y = y * _silu(z_ref[:, gc_])
        y = y * lax.rsqrt(jnp.mean(y * y, -1, keepdims=True) + RMS_EPS)
        y_ref[:, gc_] = (y * normg_ref[:, gc_]).astype(y_ref.dtype)

    cw = min(512, D)
    for j in range(D // cw):
        cols = slice(j * cw, (j + 1) * cw)
        extu_ref[8:8 + L, cols] = gc_ref[:, cols] * hh_ref[:, cols]
        base = 8 - (KS - 1)
        v = extu_ref[base:base + L, cols] * scw_ref[0:1, cols]
        for k in range(1, KS):
            v = v + extu_ref[base + k:base + k + L, cols] * scw_ref[k:k + 1, cols]
        y_ref[:, D + j * cw:D + (j + 1) * cw] = (gb_ref[:, cols] * v).astype(y_ref.dtype)

    @pl.when(c == nc - 1)
    def _fin():
        utail_ref[0] = extu_ref[L:L + 8, :]
        for g in range(G):
            if has_init:
                hout_ref[0, g * R:(g + 1) * R] = state_ref[g].reshape(R, P, N)
            else:
                hout_ref[0, g * R:(g + 1) * R] = state_ref[g].T.reshape(R, P, N)

    ext_ref[0:8, :] = ext_ref[L:L + 8, :]
    extu_ref[0:8, :] = extu_ref[L:L + 8, :]


def even_mixer(proj, dt_raw, row0, nb, T, cfg, wts, init=None):
    D, G, R, P, N = cfg["D"], cfg["G"], cfg["R"], cfg["P"], cfg["N"]
    H = G * R
    GN = G * N
    CH = D + 2 * GN
    L = SSD_CHUNK if T % SSD_CHUNK == 0 else T
    nc = T // L
    assert row0 % L == 0 and L % 8 == 0 and 2 * GN == D and R * P * G == D
    rb0 = row0 // L
    has_init = init is not None
    convw, convb, dtb, alog, dskip, normg, scw = wts
    tril = jnp.tril(jnp.ones((L, L), F32))
    expand = (jnp.arange(V7X_LANES)[:, None] == (jnp.arange(D) // P)[None, :]).astype(F32)

    def col_spec(j):
        return pl.BlockSpec((L, D), lambda b, c: (rb0 + b * nc + c, j))

    def const_spec(shape):
        return pl.BlockSpec(shape, lambda b, c: (0,) * len(shape))

    in_specs = [col_spec(0), col_spec(1), col_spec(2), col_spec(3), col_spec(4), col_spec(5),
                pl.BlockSpec((L, V7X_LANES), lambda b, c: (rb0 + b * nc + c, 0)),
                const_spec(convw.shape), const_spec(convb.shape), const_spec(dtb.shape), const_spec(alog.shape),
                const_spec(dskip.shape), const_spec(normg.shape), const_spec(scw.shape),
                const_spec(tril.shape), const_spec(expand.shape)]
    args = [proj] * 6 + [dt_raw, convw, convb, dtb, alog, dskip, normg, scw, tril, expand]
    if has_init:
        h0, ctx, scctx = init
        in_specs += [pl.BlockSpec((1, H, P, N), lambda b, c: (b, 0, 0, 0)),
                     pl.BlockSpec((1,) + ctx.shape[1:], lambda b, c: (b, 0, 0)),
                     pl.BlockSpec((1,) + scctx.shape[1:], lambda b, c: (b, 0, 0))]
        args += [h0, ctx, scctx]
    state_shape = (G, R * P, N) if has_init else (G, N, R * P)
    return pl.pallas_call(
        functools.partial(_even_mixer_kernel, L=L, nc=nc, G=G, R=R, P=P, N=N, has_init=has_init),
        grid=(nb, nc),
        in_specs=in_specs,
        out_specs=[pl.BlockSpec((L, 2 * D), lambda b, c: (b * nc + c, 0)),
                   pl.BlockSpec((1, H, P, N), lambda b, c: (b, 0, 0, 0)),
                   pl.BlockSpec((1, 8, D), lambda b, c: (b, 0, 0))],
        out_shape=[jax.ShapeDtypeStruct((nb * T, 2 * D), BF16 if L % 16 == 0 else F32),
                   jax.ShapeDtypeStruct((nb, H, P, N), F32),
                   jax.ShapeDtypeStruct((nb, 8, D), F32)],
        scratch_shapes=[pltpu.VMEM(state_shape, F32), pltpu.VMEM((L + 8, CH), F32),
                        pltpu.VMEM((L, CH), F32), pltpu.VMEM((L + 8, D), F32)],
        compiler_params=_params("parallel", "arbitrary"),
    )(*args)


def _pool_sample_kernel(st_ref, x_ref, p_ref, ext_ref, *, start_pos, dg):
    T = x_ref.shape[1]
    ext_ref[:, 16 - POOL_CTX:16, :] = st_ref[...]
    ext_ref[:, 16:16 + T, :] = x_ref[...]
    pos = start_pos + lax.broadcasted_iota(jnp.int32, (1, T, 1), 1)
    for gi, w in enumerate(POOL_WINDOWS):
        cols = slice(gi * dg, (gi + 1) * dg)
        s = ext_ref[:, 16:16 + T, cols]
        for j in range(1, w):
            s = s + ext_ref[:, 16 - j:16 - j + T, cols]
        cnt = jnp.minimum(pos + 1, w).astype(F32)
        p_ref[:, :, cols] = (s / cnt - x_ref[:, :, cols]).astype(BF16)


def pool_sample(st, x3, dg):
    B, T, D = x3.shape
    bb = _tile(B, 16, 1)
    return pl.pallas_call(
        functools.partial(_pool_sample_kernel, start_pos=PAST_LEN, dg=dg),
        grid=(B // bb,),
        in_specs=[pl.BlockSpec((bb, POOL_CTX, D), lambda i: (i, 0, 0)),
                  pl.BlockSpec((bb, T, D), lambda i: (i, 0, 0))],
        out_specs=pl.BlockSpec((bb, T, D), lambda i: (i, 0, 0)),
        out_shape=jax.ShapeDtypeStruct((B, T, D), BF16),
        scratch_shapes=[pltpu.VMEM((bb, 16 + T, D), F32)],
        compiler_params=_params("parallel"),
    )(st, x3)


def _pool_ln_kernel(x_ref, halo_ref, ps_ref, w_ref, scale_ref, g_ref, b_ref, of_ref, ob_ref, ext_ref, p_ref,
                    *, n_prompt_tiles, tiles_per_seq, dg, alpha):
    i = pl.program_id(0)
    bm = x_ref.shape[0]

    @pl.when(i < n_prompt_tiles)
    def _prompt():
        first = (i % tiles_per_seq) == 0
        ext_ref[0:16, :] = jnp.where(first, 0.0, halo_ref[...])
        ext_ref[16:16 + bm, :] = x_ref[...]
        pos = (i % tiles_per_seq) * bm + lax.broadcasted_iota(jnp.int32, (bm, 1), 0)
        for gi, w in enumerate(POOL_WINDOWS):
            cols = slice(gi * dg, (gi + 1) * dg)
            s = ext_ref[16:16 + bm, cols]
            for j in range(1, w):
                s = s + ext_ref[16 - j:16 - j + bm, cols]
            cnt = jnp.minimum(pos + 1, w).astype(F32)
            p_ref[:, cols] = (s / cnt - x_ref[:, cols]).astype(BF16)

    @pl.when(i >= n_prompt_tiles)
    def _sample():
        p_ref[...] = ps_ref[...]

    for gi in range(len(POOL_WINDOWS)):
        cols = slice(gi * dg, (gi + 1) * dg)
        m = jnp.dot(p_ref[:, cols], w_ref[gi], preferred_element_type=F32) * scale_ref[:, cols]
        ext_ref[16:16 + bm, cols] = alpha * x_ref[:, cols] + m
    out = _layer_norm(ext_ref[16:16 + bm, :], g_ref[...], b_ref[...])
    of_ref[...] = out
    ob_ref[...] = out.astype(BF16)


def pool_ln(x, p_sample, w_pool, scale, g, b, n_prompt_rows, T, alpha, bm=512):
    M, D = x.shape
    dg = D // len(POOL_WINDOWS)
    bm = _tile(math.gcd(math.gcd(n_prompt_rows, M - n_prompt_rows), T), bm, 16)
    npt = n_prompt_rows // bm
    hb = bm // 16
    return pl.pallas_call(
        functools.partial(_pool_ln_kernel, n_prompt_tiles=npt, tiles_per_seq=T // bm, dg=dg, alpha=alpha),
        grid=(M // bm,),
        in_specs=[pl.BlockSpec((bm, D), lambda i: (i, 0)),
                  pl.BlockSpec((16, D), lambda i: (jnp.maximum(i * hb - 1, 0), 0)),
                  pl.BlockSpec((bm, D), lambda i: (jnp.maximum(i - npt, 0), 0)),
                  pl.BlockSpec(w_pool.shape, lambda i: (0, 0, 0)),
                  pl.BlockSpec((1, D), lambda i: (0, 0)),
                  pl.BlockSpec((1, D), lambda i: (0, 0)),
                  pl.BlockSpec((1, D), lambda i: (0, 0))],
        out_specs=[pl.BlockSpec((bm, D), lambda i: (i, 0)), pl.BlockSpec((bm, D), lambda i: (i, 0))],
        out_shape=[jax.ShapeDtypeStruct((M, D), F32), jax.ShapeDtypeStruct((M, D), BF16)],
        scratch_shapes=[pltpu.VMEM((bm + 16, D), F32), pltpu.VMEM((bm, D), BF16)],
        compiler_params=_params("arbitrary"),
    )(x, x, p_sample, w_pool, scale, g, b)


def _softmax_rows(s):
    m = jnp.max(s, -1, keepdims=True)
    e = jnp.exp(s - m)
    return e / jnp.sum(e, -1, keepdims=True)


def _attn_prompt_kernel(q_ref, k_ref, v_ref, o_ref, *, scale):
    s = lax.dot_general(q_ref[...], k_ref[...].astype(BF16), (((1,), (1,)), ((), ())),
                        preferred_element_type=F32) * scale
    p = _softmax_rows(s).astype(BF16)
    o_ref[...] = jnp.dot(p, v_ref[...].astype(BF16), preferred_element_type=F32).astype(BF16)


def attn_prompt(q, mk, mv, nb, T, mem_len, heads, tq=1024):
    D = q.shape[1]
    hd = D // heads
    tq = _tile(T, tq, V7X_SUBLANES)
    nt = T // tq
    return pl.pallas_call(
        functools.partial(_attn_prompt_kernel, scale=hd ** -0.5),
        grid=(nb, heads, nt),
        in_specs=[pl.BlockSpec((tq, hd), lambda b, h, t: (b * nt + t, h)),
                  pl.BlockSpec((mem_len, hd), lambda b, h, t: (b, h)),
                  pl.BlockSpec((mem_len, hd), lambda b, h, t: (b, h))],
        out_specs=pl.BlockSpec((tq, hd), lambda b, h, t: (b * nt + t, h)),
        out_shape=jax.ShapeDtypeStruct((nb * T, D), BF16),
        compiler_params=_params("parallel", "parallel", "parallel"),
    )(q, mk, mv)


def _attn_sample_kernel(q_ref, k_ref, v_ref, o_ref, *, scale, heads, T):
    bb = k_ref.shape[0]
    hd = k_ref.shape[2] // heads
    q_all = q_ref[...].astype(F32)
    rows = []
    for b in range(bb):
        outs = []
        for h in range(heads):
            cols = slice(h * hd, (h + 1) * hd)
            q = q_all[b * T:(b + 1) * T, cols].astype(BF16)
            s = lax.dot_general(q, k_ref[b, :, cols].astype(BF16), (((1,), (1,)), ((), ())),
                                preferred_element_type=F32) * scale
            p = _softmax_rows(s).astype(BF16)
            outs.append(jnp.dot(p, v_ref[b, :, cols].astype(BF16), preferred_element_type=F32))
        rows.append(jnp.concatenate(outs, axis=1))
    o_ref[...] = jnp.concatenate(rows, axis=0).astype(BF16)


def attn_sample(q, cache_k, cache_v, layer, row0, T, heads, bb=2):
    _, B, mem_len, D = cache_k.shape
    bb = _tile(B, bb, 1)
    rows = bb * T
    assert row0 % rows == 0
    rb0 = row0 // rows
    return pl.pallas_call(
        functools.partial(_attn_sample_kernel, scale=(D // heads) ** -0.5, heads=heads, T=T),
        grid=(B // bb,),
        in_specs=[pl.BlockSpec((rows, D), lambda i: (rb0 + i, 0)),
                  pl.BlockSpec((None, bb, mem_len, D), lambda i: (layer, i, 0, 0)),
                  pl.BlockSpec((None, bb, mem_len, D), lambda i: (layer, i, 0, 0))],
        out_specs=pl.BlockSpec((rows, D), lambda i: (i, 0)),
        out_shape=jax.ShapeDtypeStruct((B * T, D), BF16),
        compiler_params=_params("parallel"),
    )(q, cache_k, cache_v)


def kernel(x_prompt, x_sample, state_ssd, state_ssd_conv, state_short_conv, state_pool, cache_mem_k, cache_mem_v, mem_prompt, w_in_even, ssd_conv_w, ssd_conv_b, ssd_dt_bias, ssd_a_log, ssd_d, ssd_norm_g, sc_conv_w, w_out_even, w_pool, pool_scale, wq_x, wk_x, wv_x, wo_x, w_up, w_down, ln_g, ln_b):
    BP, T, D = x_prompt.shape
    BS, TS, _ = x_sample.shape
    depth = wq_x.shape[0]
    _, _, H, P, N = state_ssd.shape
    CH = state_ssd_conv.shape[-1]
    G = (CH - D) // (2 * N)
    R = H // G
    cfg = dict(D=D, G=G, R=R, P=P, N=N)
    mem_len, heads, hd = cache_mem_k.shape[2:]
    dg = w_pool.shape[-1]
    alpha = (2.0 * depth) ** 0.25
    MP, MS = BP * T, BS * TS
    o_dt = D + CH
    assert ssd_d.shape[-1] == H and H <= V7X_LANES and heads * hd == D

    xf = jnp.concatenate([x_prompt.reshape(MP, D), x_sample.reshape(MS, D)], axis=0)
    xb = xf.astype(BF16)
    cache_k = cache_mem_k.reshape(depth, BS, mem_len, D)
    cache_v = cache_mem_v.reshape(depth, BS, mem_len, D)
    mem_b = mem_prompt.reshape(BP * mem_len, D).astype(BF16)

    def pad_lanes(v):
        return jnp.pad(v.astype(F32), (0, V7X_LANES - v.shape[0]))[None, :]

    hs_p, hs_s, cbs_p, cbs_s, sbs_p, sbs_s, pbs_p, pbs_s, mks, mvs = [], [], [], [], [], [], [], [], [], []
    for layer in range(depth):
        if layer % 2 == 0:
            e = layer // 2
            w_in = w_in_even[e]
            w_main = jnp.concatenate([w_in[:, :o_dt], w_in[:, o_dt + H:]], axis=1).astype(BF16)
            w_dt = jnp.pad(w_in[:, o_dt:o_dt + H], ((0, 0), (0, V7X_LANES - H))).astype(BF16)
            proj = matmul(xb, w_main, F32)
            dt_raw = matmul(xb, w_dt, F32)
            wts = (ssd_conv_w[e], ssd_conv_b[e][None, :], pad_lanes(ssd_dt_bias[e]), pad_lanes(ssd_a_log[e]),
                   jnp.repeat(ssd_d[e].astype(F32), P)[None, :], ssd_norm_g[e][None, :], sc_conv_w[e])
            y_p, h_p, ut_p = even_mixer(proj, dt_raw, 0, BP, T, cfg, wts)
            y_s, h_s, ut_s = even_mixer(proj, dt_raw, MP, BS, TS, cfg, wts,
                                        init=(state_ssd[e], state_ssd_conv[e], state_short_conv[e]))
            hs_p.append(h_p)
            hs_s.append(h_s)
            kc, ks = ssd_conv_w.shape[1], sc_conv_w.shape[1]
            cbs_p.append(proj[:MP].reshape(BP, T, -1)[:, T - (kc - 1):, D:D + CH])
            cbs_s.append(proj[MP:].reshape(BS, TS, -1)[:, TS - (kc - 1):, D:D + CH])
            sbs_p.append(ut_p[:, 8 - (ks - 1):])
            sbs_s.append(ut_s[:, 8 - (ks - 1):])
            y_cat = jnp.concatenate([y_p, y_s.astype(BF16)], axis=0)
            xf, xb = matmul_ln(y_cat, w_out_even[e].astype(BF16), xf,
                               ln_g[layer, 0][None, :], ln_b[layer, 0][None, :], alpha)
        else:
            o = layer // 2
            xp3 = xf[:MP].reshape(BP, T, D)
            xs3 = xf[MP:].reshape(BS, TS, D)
            pbs_p.append(xp3[:, T - POOL_CTX:])
            pbs_s.append(jnp.concatenate([state_pool[o][:, TS:], xs3], axis=1))
            p_s = pool_sample(state_pool[o], xs3, dg).reshape(MS, D)
            xf, xb = pool_ln(xf, p_s, w_pool[o].astype(BF16), pool_scale[o][None, :],
                             ln_g[layer, 0][None, :], ln_b[layer, 0][None, :], MP, T, alpha)

        mk = matmul(mem_b, wk_x[layer].astype(BF16), F32)
        mv = matmul(mem_b, wv_x[layer].astype(BF16), F32)
        mks.append(mk)
        mvs.append(mv)
        q = matmul(xb, wq_x[layer].astype(BF16), BF16)
        o_p = attn_prompt(q, mk, mv, BP, T, mem_len, heads)
        o_s = attn_sample(q, cache_k, cache_v, layer, MP, TS, heads)
        xf, xb = matmul_ln(jnp.concatenate([o_p, o_s], axis=0), wo_x[layer].astype(BF16), xf,
                           ln_g[layer, 1][None, :], ln_b[layer, 1][None, :], alpha)

        hid = matmul(xb, w_up[layer].astype(BF16), BF16, act="relu2")
        xf, xb = matmul_ln(hid, w_down[layer].astype(BF16), xf,
                           ln_g[layer, 2][None, :], ln_b[layer, 2][None, :], alpha)

    kv_shape = (depth, BP, mem_len, heads, hd)
    return (xf[:MP].reshape(BP, T, D), xf[MP:].reshape(BS, TS, D),
            jnp.stack(hs_p), jnp.stack(hs_s), jnp.stack(cbs_p), jnp.stack(cbs_s),
            jnp.stack(sbs_p), jnp.stack(sbs_s), jnp.stack(pbs_p), jnp.stack(pbs_s),
            jnp.stack(mks).reshape(kv_shape), jnp.stack(mvs).reshape(kv_shape))
```

```python
import functools
import math

import jax
import jax.numpy as jnp
from jax import lax
from jax.experimental import pallas as pl
from jax.experimental.pallas import tpu as pltpu

F32 = jnp.float32
BF16 = jnp.bfloat16
HIGHEST = lax.Precision.HIGHEST

PAST_LEN = 16384
SSD_CHUNK = 128
POOL_WINDOWS = (2, 4, 8, 16)
POOL_CTX = 15
LN_EPS = 1e-5
RMS_EPS = 1e-5

V7X_LANES = 128
V7X_SUBLANES = 8
V7X_VMEM_BYTES = 64 * 1024 * 1024
VMEM_LIMIT = V7X_VMEM_BYTES - 8 * 1024 * 1024


def _tile(dim, pref, quantum):
    best = None
    t = quantum
    while t <= min(dim, pref):
        if dim % t == 0:
            best = t
        t += quantum
    return best if best is not None else dim


def _params(*sem):
    return pltpu.CompilerParams(dimension_semantics=sem, vmem_limit_bytes=VMEM_LIMIT)


def _silu(x):
    return x * jax.nn.sigmoid(x)


def _softplus(x):
    return jnp.maximum(x, 0.0) + jnp.log1p(jnp.exp(-jnp.abs(x)))


def _layer_norm(y, g, b):
    mu = jnp.mean(y, -1, keepdims=True)
    d = y - mu
    var = jnp.mean(d * d, -1, keepdims=True)
    return d * lax.rsqrt(var + LN_EPS) * g + b


def _mm_kernel(x_ref, w_ref, o_ref, wb_ref, *, act):
    @pl.when(pl.program_id(1) == 0)
    def _():
        wb_ref[...] = w_ref[...].astype(BF16)

    acc = jnp.dot(x_ref[...], wb_ref[...], preferred_element_type=F32)
    if act == "relu2":
        acc = jnp.square(jnp.maximum(acc, 0.0))
    o_ref[...] = acc.astype(o_ref.dtype)


def matmul(x, w, layer, out_dtype, act=None, n_cols=None, bm=1024, bn=1024):
    M, K = x.shape
    N = w.shape[2] if n_cols is None else n_cols
    bm = _tile(M, bm, V7X_SUBLANES)
    bn = _tile(N, bn, V7X_LANES)
    return pl.pallas_call(
        functools.partial(_mm_kernel, act=act),
        grid=(N // bn, M // bm),
        in_specs=[pl.BlockSpec((bm, K), lambda j, i: (i, 0)),
                  pl.BlockSpec((None, K, bn), lambda j, i: (layer, 0, j))],
        out_specs=pl.BlockSpec((bm, bn), lambda j, i: (i, j)),
        out_shape=jax.ShapeDtypeStruct((M, N), out_dtype),
        scratch_shapes=[pltpu.VMEM((K, bn), BF16)],
        compiler_params=_params("parallel", "arbitrary"),
    )(x, w)


def _kv_proj_kernel(x_ref, w_ref, o2_ref, o5_ref, wb_ref, *, heads):
    @pl.when(pl.program_id(1) == 0)
    def _():
        wb_ref[...] = w_ref[...].astype(BF16)

    acc = jnp.dot(x_ref[...], wb_ref[...], preferred_element_type=F32)
    o2_ref[...] = acc
    hd = acc.shape[1] // heads
    for h in range(heads):
        o5_ref[0, :, h, :] = acc[:, h * hd:(h + 1) * hd]


def kv_proj(mem_b, w, nb, heads):
    M, K = mem_b.shape
    depth, _, N = w.shape
    mem_len = M // nb
    hd = N // heads
    return pl.pallas_call(
        functools.partial(_kv_proj_kernel, heads=heads),
        grid=(depth, nb),
        in_specs=[pl.BlockSpec((mem_len, K), lambda l, b: (b, 0)),
                  pl.BlockSpec((None, K, N), lambda l, b: (l, 0, 0), pipeline_mode=pl.Buffered(1))],
        out_specs=[pl.BlockSpec((None, mem_len, N), lambda l, b: (l, b, 0)),
                   pl.BlockSpec((None, 1, mem_len, heads, hd), lambda l, b: (l, b, 0, 0, 0))],
        out_shape=[jax.ShapeDtypeStruct((depth, M, N), F32),
                   jax.ShapeDtypeStruct((depth, nb, mem_len, heads, hd), F32)],
        scratch_shapes=[pltpu.VMEM((K, N), BF16)],
        compiler_params=_params("parallel", "arbitrary"),
    )(mem_b, w)


def _mm_ln_kernel(x_ref, w_ref, res_ref, g_ref, b_ref, of_ref, ob_ref, *, alpha):
    m = jnp.dot(x_ref[...], w_ref[...], preferred_element_type=F32)
    out = _layer_norm(alpha * res_ref[...] + m, g_ref[...], b_ref[...])
    of_ref[...] = out
    ob_ref[...] = out.astype(BF16)


def matmul_ln(x, w, res, g, b, alpha):
    M, K = x.shape
    N = w.shape[1]
    w_bytes = K * N * 2
    per_row = 2 * (K * 2) + 2 * (N * 4) + 2 * (N * 4) + 2 * (N * 2) + 2 * (N * 4)
    bm = _tile(M, min(512, (VMEM_LIMIT - w_bytes) // per_row), 2 * V7X_SUBLANES)
    return pl.pallas_call(
        functools.partial(_mm_ln_kernel, alpha=alpha),
        grid=(M // bm,),
        in_specs=[pl.BlockSpec((bm, K), lambda i: (i, 0)),
                  pl.BlockSpec((K, N), lambda i: (0, 0), pipeline_mode=pl.Buffered(1)),
                  pl.BlockSpec((bm, N), lambda i: (i, 0)),
                  pl.BlockSpec((1, N), lambda i: (0, 0)),
                  pl.BlockSpec((1, N), lambda i: (0, 0))],
        out_specs=[pl.BlockSpec((bm, N), lambda i: (i, 0)),
                   pl.BlockSpec((bm, N), lambda i: (i, 0))],
        out_shape=[jax.ShapeDtypeStruct((M, N), F32), jax.ShapeDtypeStruct((M, N), BF16)],
        compiler_params=_params("parallel"),
    )(x, w, res, g, b)


def _even_mixer_kernel(*refs, L, nc, G, R, P, N, has_init, n_prev):
    (z_ref, xs_ref, bc_ref, gb_ref, gc_ref, hh_ref, dt_ref, convw_ref, convb_ref, dtb_ref, alog_ref,
     dskip_ref, normg_ref, scw_ref, tril_ref, expand_ref) = refs[:16]
    pos = 16
    if has_init:
        h0_ref, ctx_ref, scctx_ref = refs[pos:pos + 3]
        pos += 3
    prev_refs = refs[pos:pos + n_prev]
    pos += n_prev
    y_ref, hout_ref, utail_ref, ctail_ref = refs[pos:pos + 4]
    state_ref, ext_ref, act_ref, extu_ref = refs[pos + 4:]

    RP = R * P
    D = G * RP
    GN = G * N
    CH = D + 2 * GN
    KC = convw_ref.shape[0]
    KS = scw_ref.shape[0]
    c = pl.program_id(1)

    @pl.when(c == 0)
    def _init():
        if has_init:
            ext_ref[8 - (KC - 1):8, :] = ctx_ref[0]
            extu_ref[8 - (KS - 1):8, :] = scctx_ref[0]
            for g in range(G):
                state_ref[g] = h0_ref[0, g * R:(g + 1) * R].reshape(RP, N)
        else:
            ext_ref[0:8, :] = jnp.zeros((8, CH), F32)
            extu_ref[0:8, :] = jnp.zeros((8, D), F32)
            state_ref[...] = jnp.zeros(state_ref.shape, F32)

    ext_ref[8:8 + L, 0:D] = xs_ref[...]
    ext_ref[8:8 + L, D:CH] = bc_ref[...]
    cw = min(512, CH)
    for j in range(CH // cw):
        cols = slice(j * cw, (j + 1) * cw)
        base = 8 - (KC - 1)
        acc = ext_ref[base:base + L, cols] * convw_ref[0:1, cols]
        for k in range(1, KC):
            acc = acc + ext_ref[base + k:base + k + L, cols] * convw_ref[k:k + 1, cols]
        act_ref[:, cols] = _silu(acc + convb_ref[:, cols])

    dt = _softplus(dt_ref[...] + dtb_ref[...])
    a = dt * (-jnp.exp(alog_ref[...]))
    acs = jnp.dot(tril_ref[...], a, precision=HIGHEST, preferred_element_type=F32)
    dt_e = jnp.dot(dt, expand_ref[...], precision=HIGHEST, preferred_element_type=F32)
    acs_e = jnp.dot(acs, expand_ref[...], precision=HIGHEST, preferred_element_type=F32)
    if L == V7X_LANES:
        acs_row = acs.T
    else:
        acs_row = jnp.concatenate([acs, jnp.zeros((V7X_LANES - L, V7X_LANES), F32)], axis=0).T[:, 0:L]

    li = lax.broadcasted_iota(jnp.int32, (L, L), 0)
    si = lax.broadcasted_iota(jnp.int32, (L, L), 1)
    causal = si <= li
    ci = lax.broadcasted_iota(jnp.int32, (L, RP), 1)
    head_cols = [jnp.logical_and(ci >= r * P, ci < (r + 1) * P) for r in range(R)]

    for g in range(G):
        gc_ = slice(g * RP, (g + 1) * RP)
        xs_g = act_ref[:, gc_]
        b_g = act_ref[:, D + g * N:D + (g + 1) * N]
        c_g = act_ref[:, D + GN + g * N:D + GN + (g + 1) * N].astype(BF16)
        xdt = xs_g * dt_e[:, gc_]
        x_bd = jnp.concatenate([jnp.where(head_cols[r], xdt, 0.0) for r in range(R)], axis=0).astype(BF16)
        acs_g = acs_e[:, gc_]
        if has_init:
            cb = lax.dot_general(c_g, b_g.astype(BF16), (((1,), (1,)), ((), ())), preferred_element_type=F32)
        else:
            bt_g = b_g.T
            cb = jnp.dot(c_g, bt_g.astype(BF16), preferred_element_type=F32)

        ms, dbs = [], []
        for r in range(R):
            h = g * R + r
            col = acs[:, h:h + 1]
            row = acs_row[h:h + 1, :]
            decay = jnp.exp(jnp.where(causal, col - row, -jnp.inf))
            ms.append(cb * decay)
            if has_init:
                dbs.append(jnp.exp(acs[L - 1:L, h:h + 1] - col) * b_g)
            else:
                dbs.append(bt_g * jnp.exp(acs_row[h:h + 1, L - 1:L] - row))

        if L % V7X_LANES == 0:
            y = jnp.dot(jnp.concatenate(ms, axis=1).astype(BF16), x_bd, preferred_element_type=F32)
        else:
            y = jnp.zeros((L, RP), F32)
            for r in range(R):
                y = y + jnp.dot(ms[r].astype(BF16), x_bd[r * L:(r + 1) * L], preferred_element_type=F32)

        h_prev = state_ref[g]
        if has_init:
            y_off = lax.dot_general(c_g, h_prev.astype(BF16), (((1,), (1,)), ((), ())),
                                    preferred_element_type=F32)
            db = jnp.concatenate(dbs, axis=0).astype(BF16)
            new_states = lax.dot_general(x_bd, db, (((0,), (0,)), ((), ())), preferred_element_type=F32)
            cd = jnp.concatenate(
                [jnp.broadcast_to(jnp.exp(acs[L - 1:L, g * R + r:g * R + r + 1]), (P, N)) for r in range(R)], axis=0)
            state_ref[g] = h_prev * cd + new_states
        else:
            y_off = jnp.dot(c_g, h_prev.astype(BF16), preferred_element_type=F32)
            dbt = jnp.concatenate(dbs, axis=1).astype(BF16)
            new_states = jnp.dot(dbt, x_bd, preferred_element_type=F32)
            state_ref[g] = h_prev * jnp.exp(acs_g[L - 1:L, :]) + new_states
        y = y + y_off * jnp.exp(acs_g)
        y = y + dskip_ref[:, gc_] * xs_g
        y = y * _silu(z_ref[:, gc_])
        y = y * lax.rsqrt(jnp.mean(y * y, -1, keepdims=True) + RMS_EPS)
        y_ref[:, gc_] = (y * normg_ref[:, gc_]).astype(y_ref.dtype)

    cw = min(512, D)
    for j in range(D // cw):
        cols = slice(j * cw, (j + 1) * cw)
        extu_ref[8:8 + L, cols] = gc_ref[:, cols] * hh_ref[:, cols]
        base = 8 - (KS - 1)
        v = extu_ref[base:base + L, cols] * scw_ref[0:1, cols]
        for k in range(1, KS):
            v = v + extu_ref[base + k:base + k + L, cols] * scw_ref[k:k + 1, cols]
        y_ref[:, D + j * cw:D + (j + 1) * cw] = (gb_ref[:, cols] * v).astype(y_ref.dtype)

    @pl.when(c == nc - 1)
    def _fin():
        utail_ref[0] = extu_ref[L:L + 8, :]
        ctail_ref[0] = ext_ref[L:L + 8, :]
        for s, prev_ref in enumerate(prev_refs):
            hout_ref[s] = prev_ref[...]
        for g in range(G):
            if has_init:
                hout_ref[n_prev, 0, g * R:(g + 1) * R] = state_ref[g].reshape(R, P, N)
            else:
                hout_ref[n_prev, 0, g * R:(g + 1) * R] = state_ref[g].T.reshape(R, P, N)

    ext_ref[0:8, :] = ext_ref[L:L + 8, :]
    extu_ref[0:8, :] = extu_ref[L:L + 8, :]


def even_mixer(proj_a, proj_b, dt_raw, row0, nb, T, cfg, wts, init=None, prev_states=()):
    D, G, R, P, N = cfg["D"], cfg["G"], cfg["R"], cfg["P"], cfg["N"]
    H = G * R
    GN = G * N
    CH = D + 2 * GN
    L = SSD_CHUNK if T % SSD_CHUNK == 0 else T
    nc = T // L
    assert row0 % L == 0 and L % 8 == 0 and 2 * GN == D and R * P * G == D
    rb0 = row0 // L
    has_init = init is not None
    convw, convb, dtb, alog, dskip, normg, scw = wts
    tril = jnp.tril(jnp.ones((L, L), F32))
    expand = (jnp.arange(V7X_LANES)[:, None] == (jnp.arange(D) // P)[None, :]).astype(F32)

    def col_spec(j):
        return pl.BlockSpec((L, D), lambda b, c: (rb0 + b * nc + c, j))

    def const_spec(shape):
        return pl.BlockSpec(shape, lambda b, c: (0,) * len(shape))

    in_specs = [col_spec(0), col_spec(1), col_spec(2), col_spec(0), col_spec(1), col_spec(2),
                pl.BlockSpec((L, V7X_LANES), lambda b, c: (rb0 + b * nc + c, 0)),
                const_spec(convw.shape), const_spec(convb.shape), const_spec(dtb.shape), const_spec(alog.shape),
                const_spec(dskip.shape), const_spec(normg.shape), const_spec(scw.shape),
                const_spec(tril.shape), const_spec(expand.shape)]
    args = [proj_a] * 3 + [proj_b] * 3 + [dt_raw, convw, convb, dtb, alog, dskip, normg, scw, tril, expand]
    if has_init:
        h0, ctx, scctx = init
        in_specs += [pl.BlockSpec((1, H, P, N), lambda b, c: (b, 0, 0, 0)),
                     pl.BlockSpec((1,) + ctx.shape[1:], lambda b, c: (b, 0, 0)),
                     pl.BlockSpec((1,) + scctx.shape[1:], lambda b, c: (b, 0, 0))]
        args += [h0, ctx, scctx]
    n_prev = len(prev_states)
    in_specs += [pl.BlockSpec((1, H, P, N), lambda b, c: (b, 0, 0, 0))] * n_prev
    args += list(prev_states)
    n_slots = n_prev + 1
    state_shape = (G, R * P, N) if has_init else (G, N, R * P)
    return pl.pallas_call(
        functools.partial(_even_mixer_kernel, L=L, nc=nc, G=G, R=R, P=P, N=N, has_init=has_init, n_prev=n_prev),
        grid=(nb, nc),
        in_specs=in_specs,
        out_specs=[pl.BlockSpec((L, 2 * D), lambda b, c: (b * nc + c, 0)),
                   pl.BlockSpec((n_slots, 1, H, P, N), lambda b, c: (0, b, 0, 0, 0)),
                   pl.BlockSpec((1, 8, D), lambda b, c: (b, 0, 0)),
                   pl.BlockSpec((1, 8, CH), lambda b, c: (b, 0, 0))],
        out_shape=[jax.ShapeDtypeStruct((nb * T, 2 * D), BF16 if L % 16 == 0 else F32),
                   jax.ShapeDtypeStruct((n_slots, nb, H, P, N), F32),
                   jax.ShapeDtypeStruct((nb, 8, D), F32),
                   jax.ShapeDtypeStruct((nb, 8, CH), F32)],
        scratch_shapes=[pltpu.VMEM(state_shape, F32), pltpu.VMEM((L + 8, CH), F32),
                        pltpu.VMEM((L, CH), F32), pltpu.VMEM((L + 8, D), F32)],
        compiler_params=_params("parallel", "arbitrary"),
    )(*args)


def _pool_sample_kernel(st_ref, x_ref, p_ref, ext_ref, *, start_pos, dg):
    T = x_ref.shape[1]
    ext_ref[:, 16 - POOL_CTX:16, :] = st_ref[...]
    ext_ref[:, 16:16 + T, :] = x_ref[...]
    pos = start_pos + lax.broadcasted_iota(jnp.int32, (1, T, 1), 1)
    for gi, w in enumerate(POOL_WINDOWS):
        cols = slice(gi * dg, (gi + 1) * dg)
        s = ext_ref[:, 16:16 + T, cols]
        for j in range(1, w):
            s = s + ext_ref[:, 16 - j:16 - j + T, cols]
        cnt = jnp.minimum(pos + 1, w).astype(F32)
        p_ref[:, :, cols] = (s / cnt - x_ref[:, :, cols]).astype(BF16)


def pool_sample(st, x3, dg):
    B, T, D = x3.shape
    bb = _tile(B, 16, 1)
    return pl.pallas_call(
        functools.partial(_pool_sample_kernel, start_pos=PAST_LEN, dg=dg),
        grid=(B // bb,),
        in_specs=[pl.BlockSpec((bb, POOL_CTX, D), lambda i: (i, 0, 0)),
                  pl.BlockSpec((bb, T, D), lambda i: (i, 0, 0))],
        out_specs=pl.BlockSpec((bb, T, D), lambda i: (i, 0, 0)),
        out_shape=jax.ShapeDtypeStruct((B, T, D), BF16),
        scratch_shapes=[pltpu.VMEM((bb, 16 + T, D), F32)],
        compiler_params=_params("parallel"),
    )(st, x3)


def _pool_ln_kernel(x_ref, halo_ref, ps_ref, w_ref, scale_ref, g_ref, b_ref, of_ref, ob_ref, ext_ref, p_ref,
                    *, n_prompt_tiles, tiles_per_seq, dg, alpha):
    i = pl.program_id(0)
    bm = x_ref.shape[0]

    @pl.when(i < n_prompt_tiles)
    def _prompt():
        first = (i % tiles_per_seq) == 0
        ext_ref[0:16, :] = jnp.where(first, 0.0, halo_ref[...])
        ext_ref[16:16 + bm, :] = x_ref[...]
        pos = (i % tiles_per_seq) * bm + lax.broadcasted_iota(jnp.int32, (bm, 1), 0)
        for gi, w in enumerate(POOL_WINDOWS):
            cols = slice(gi * dg, (gi + 1) * dg)
            s = ext_ref[16:16 + bm, cols]
            for j in range(1, w):
                s = s + ext_ref[16 - j:16 - j + bm, cols]
            cnt = jnp.minimum(pos + 1, w).astype(F32)
            p_ref[:, cols] = (s / cnt - x_ref[:, cols]).astype(BF16)

    @pl.when(i >= n_prompt_tiles)
    def _sample():
        p_ref[...] = ps_ref[...]

    for gi in range(len(POOL_WINDOWS)):
        cols = slice(gi * dg, (gi + 1) * dg)
        m = jnp.dot(p_ref[:, cols], w_ref[gi], preferred_element_type=F32) * scale_ref[:, cols]
        ext_ref[16:16 + bm, cols] = alpha * x_ref[:, cols] + m
    out = _layer_norm(ext_ref[16:16 + bm, :], g_ref[...], b_ref[...])
    of_ref[...] = out
    ob_ref[...] = out.astype(BF16)


def pool_ln(x, p_sample, w_pool, scale, g, b, n_prompt_rows, T, alpha, bm=512):
    M, D = x.shape
    dg = D // len(POOL_WINDOWS)
    bm = _tile(math.gcd(math.gcd(n_prompt_rows, M - n_prompt_rows), T), bm, 16)
    npt = n_prompt_rows // bm
    hb = bm // 16
    return pl.pallas_call(
        functools.partial(_pool_ln_kernel, n_prompt_tiles=npt, tiles_per_seq=T // bm, dg=dg, alpha=alpha),
        grid=(M // bm,),
        in_specs=[pl.BlockSpec((bm, D), lambda i: (i, 0)),
                  pl.BlockSpec((16, D), lambda i: (jnp.maximum(i * hb - 1, 0), 0)),
                  pl.BlockSpec((bm, D), lambda i: (jnp.maximum(i - npt, 0), 0)),
                  pl.BlockSpec(w_pool.shape, lambda i: (0, 0, 0)),
                  pl.BlockSpec((1, D), lambda i: (0, 0)),
                  pl.BlockSpec((1, D), lambda i: (0, 0)),
                  pl.BlockSpec((1, D), lambda i: (0, 0))],
        out_specs=[pl.BlockSpec((bm, D), lambda i: (i, 0)), pl.BlockSpec((bm, D), lambda i: (i, 0))],
        out_shape=[jax.ShapeDtypeStruct((M, D), F32), jax.ShapeDtypeStruct((M, D), BF16)],
        scratch_shapes=[pltpu.VMEM((bm + 16, D), F32), pltpu.VMEM((bm, D), BF16)],
        compiler_params=_params("arbitrary"),
    )(x, x, p_sample, w_pool, scale, g, b)


def _softmax_rows(s):
    m = jnp.max(s, -1, keepdims=True)
    e = jnp.exp(s - m)
    return e / jnp.sum(e, -1, keepdims=True)


def _attn_prompt_kernel(q_ref, k_ref, v_ref, o_ref, *, scale):
    s = lax.dot_general(q_ref[...], k_ref[...].astype(BF16), (((1,), (1,)), ((), ())),
                        preferred_element_type=F32) * scale
    p = _softmax_rows(s).astype(BF16)
    o_ref[...] = jnp.dot(p, v_ref[...].astype(BF16), preferred_element_type=F32).astype(BF16)


def attn_prompt(q, mk, mv, layer, nb, T, mem_len, heads, tq=1024):
    M, D = q.shape
    hd = D // heads
    tq = _tile(T, tq, V7X_SUBLANES)
    nt = T // tq
    return pl.pallas_call(
        functools.partial(_attn_prompt_kernel, scale=hd ** -0.5),
        grid=(nb, heads, nt),
        in_specs=[pl.BlockSpec((tq, hd), lambda b, h, t: (b * nt + t, h)),
                  pl.BlockSpec((None, mem_len, hd), lambda b, h, t: (layer, b, h)),
                  pl.BlockSpec((None, mem_len, hd), lambda b, h, t: (layer, b, h))],
        out_specs=pl.BlockSpec((tq, hd), lambda b, h, t: (b * nt + t, h)),
        out_shape=jax.ShapeDtypeStruct((M, D), BF16),
        input_output_aliases={0: 0},
        compiler_params=_params("parallel", "parallel", "parallel"),
    )(q, mk, mv)


def _attn_sample_kernel(q_ref, k_hbm, v_hbm, o_ref, kbuf, vbuf, sem, *, layer, scale, T):
    _, heads, bb, mem_len, hd = kbuf.shape
    i = pl.program_id(0)
    n = pl.num_programs(0)

    def copies(step, slot):
        cps = []
        for h in range(heads):
            for t, (src, dst) in enumerate(((k_hbm, kbuf), (v_hbm, vbuf))):
                cps.append(pltpu.make_async_copy(src.at[layer, pl.ds(step * bb, bb), :, h, :],
                                                 dst.at[slot, h], sem.at[slot, t, h]))
        return cps

    @pl.when(i == 0)
    def _():
        for cp in copies(0, 0):
            cp.start()

    slot = i % 2

    @pl.when(i + 1 < n)
    def _():
        for cp in copies(i + 1, 1 - slot):
            cp.start()

    for cp in copies(i, slot):
        cp.wait()

    slots = bb * heads
    q_all = q_ref[...].astype(F32)
    st = jnp.zeros((mem_len, slots * T), F32)
    for b in range(bb):
        for h in range(heads):
            s_ = b * heads + h
            q_bh = q_all[b * T:(b + 1) * T, h * hd:(h + 1) * hd]
            parts = []
            if s_:
                parts.append(jnp.zeros((s_ * T, hd), F32))
            parts.append(q_bh)
            if s_ + 1 < slots:
                parts.append(jnp.zeros(((slots - s_ - 1) * T, hd), F32))
            q_slot = jnp.concatenate(parts, axis=0).astype(BF16)
            st = st + lax.dot_general(kbuf[slot, h, b].astype(BF16), q_slot, (((1,), (1,)), ((), ())),
                                      preferred_element_type=F32)
    st = st * scale
    e = jnp.exp(st - jnp.max(st, 0, keepdims=True))
    p_all = (e / jnp.sum(e, 0, keepdims=True)).T
    rows = []
    for b in range(bb):
        outs = []
        for h in range(heads):
            s_ = b * heads + h
            p = p_all[s_ * T:(s_ + 1) * T, :].astype(BF16)
            outs.append(jnp.dot(p, vbuf[slot, h, b].astype(BF16), preferred_element_type=F32))
        rows.append(jnp.concatenate(outs, axis=1))
    o_ref[...] = jnp.concatenate(rows, axis=0).astype(BF16)


def attn_sample(q, cache_k, cache_v, layer, row0, T):
    _, B, mem_len, heads, hd = cache_k.shape
    M, D = q.shape
    bb = V7X_LANES // (heads * T)
    assert bb * heads * T == V7X_LANES and B % bb == 0
    rows = bb * T
    assert row0 % rows == 0 and rows % 16 == 0
    rb0 = row0 // rows
    return pl.pallas_call(
        functools.partial(_attn_sample_kernel, layer=layer, scale=hd ** -0.5, T=T),
        grid=(B // bb,),
        in_specs=[pl.BlockSpec((rows, D), lambda i: (rb0 + i, 0)),
                  pl.BlockSpec(memory_space=pl.ANY),
                  pl.BlockSpec(memory_space=pl.ANY)],
        out_specs=pl.BlockSpec((rows, D), lambda i: (rb0 + i, 0)),
        out_shape=jax.ShapeDtypeStruct((M, D), BF16),
        input_output_aliases={0: 0},
        scratch_shapes=[pltpu.VMEM((2, heads, bb, mem_len, hd), F32),
                        pltpu.VMEM((2, heads, bb, mem_len, hd), F32),
                        pltpu.SemaphoreType.DMA((2, 2, heads))],
        compiler_params=_params("arbitrary"),
    )(q, cache_k, cache_v)


def kernel(x_prompt, x_sample, state_ssd, state_ssd_conv, state_short_conv, state_pool, cache_mem_k, cache_mem_v, mem_prompt, w_in_even, ssd_conv_w, ssd_conv_b, ssd_dt_bias, ssd_a_log, ssd_d, ssd_norm_g, sc_conv_w, w_out_even, w_pool, pool_scale, wq_x, wk_x, wv_x, wo_x, w_up, w_down, ln_g, ln_b):
    BP, T, D = x_prompt.shape
    BS, TS, _ = x_sample.shape
    depth = wq_x.shape[0]
    _, _, H, P, N = state_ssd.shape
    CH = state_ssd_conv.shape[-1]
    G = (CH - D) // (2 * N)
    R = H // G
    cfg = dict(D=D, G=G, R=R, P=P, N=N)
    mem_len, heads, hd = cache_mem_k.shape[2:]
    dg = w_pool.shape[-1]
    alpha = (2.0 * depth) ** 0.25
    MP, MS = BP * T, BS * TS
    o_dt = D + CH
    assert ssd_d.shape[-1] == H and H <= V7X_LANES and heads * hd == D

    xf = jnp.concatenate([x_prompt.reshape(MP, D), x_sample.reshape(MS, D)], axis=0)
    xb = xf.astype(BF16)
    mem_b = mem_prompt.reshape(BP * mem_len, D).astype(BF16)
    mk2, mk5 = kv_proj(mem_b, wk_x, BP, heads)
    mv2, mv5 = kv_proj(mem_b, wv_x, BP, heads)

    def pad_lanes(v):
        return jnp.pad(v.astype(F32), (0, V7X_LANES - v.shape[0]))[None, :]

    def last_rows(x2, nb, t, n):
        return jnp.stack([x2[(b + 1) * t - n:(b + 1) * t] for b in range(nb)])

    n_even = w_in_even.shape[0]
    kc, ks = ssd_conv_w.shape[1], sc_conv_w.shape[1]
    hs_p, hs_s, cbs_p, cbs_s, sbs_p, sbs_s, pbs_p, pbs_s = [], [], [], [], [], [], [], []
    h_s = None
    for layer in range(depth):
        if layer % 2 == 0:
            e = layer // 2
            w_tail = w_in_even[e:e + 1, :, o_dt + H:]
            w_dt = jnp.pad(w_in_even[e:e + 1, :, o_dt:o_dt + H], ((0, 0), (0, 0), (0, V7X_LANES - H)))
            proj_a = matmul(xb, w_in_even, e, F32, n_cols=o_dt)
            proj_b = matmul(xb, w_tail, 0, F32)
            dt_raw = matmul(xb, w_dt, 0, F32)
            wts = (ssd_conv_w[e], ssd_conv_b[e][None, :], pad_lanes(ssd_dt_bias[e]), pad_lanes(ssd_a_log[e]),
                   jnp.repeat(ssd_d[e].astype(F32), P)[None, :], ssd_norm_g[e][None, :], sc_conv_w[e])
            y_p, h_p, ut_p, ct_p = even_mixer(proj_a, proj_b, dt_raw, 0, BP, T, cfg, wts)
            y_s, h_s, ut_s, ct_s = even_mixer(proj_a, proj_b, dt_raw, MP, BS, TS, cfg, wts,
                                              init=(state_ssd[e], state_ssd_conv[e], state_short_conv[e]),
                                              prev_states=hs_s if e == n_even - 1 else ())
            hs_s.append(h_s[0])
            hs_p.append(h_p[0])
            cbs_p.append(ct_p[:, 8 - (kc - 1):])
            cbs_s.append(ct_s[:, 8 - (kc - 1):])
            sbs_p.append(ut_p[:, 8 - (ks - 1):])
            sbs_s.append(ut_s[:, 8 - (ks - 1):])
            y_cat = jnp.concatenate([y_p, y_s.astype(BF16)], axis=0)
            xf, xb = matmul_ln(y_cat, w_out_even[e].astype(BF16), xf,
                               ln_g[layer, 0][None, :], ln_b[layer, 0][None, :], alpha)
        else:
            o = layer // 2
            xs3 = xf[MP:].reshape(BS, TS, D)
            pbs_p.append(last_rows(xf, BP, T, POOL_CTX))
            pbs_s.append(jnp.concatenate([state_pool[o][:, TS:], xs3], axis=1))
            p_s = pool_sample(state_pool[o], xs3, dg).reshape(MS, D)
            xf, xb = pool_ln(xf, p_s, w_pool[o].astype(BF16), pool_scale[o][None, :],
                             ln_g[layer, 0][None, :], ln_b[layer, 0][None, :], MP, T, alpha)

        q = matmul(xb, wq_x, layer, BF16)
        att = attn_prompt(q, mk2, mv2, layer, BP, T, mem_len, heads)
        att = attn_sample(att, cache_mem_k, cache_mem_v, layer, MP, TS)
        xf, xb = matmul_ln(att, wo_x[layer].astype(BF16), xf,
                           ln_g[layer, 1][None, :], ln_b[layer, 1][None, :], alpha)

        hid = matmul(xb, w_up, layer, BF16, act="relu2")
        xf, xb = matmul_ln(hid, w_down[layer].astype(BF16), xf,
                           ln_g[layer, 2][None, :], ln_b[layer, 2][None, :], alpha)

    return (xf[:MP].reshape(BP, T, D), xf[MP:].reshape(BS, TS, D),
            jnp.stack(hs_p), h_s, jnp.stack(cbs_p), jnp.stack(cbs_s),
            jnp.stack(sbs_p), jnp.stack(sbs_s), jnp.stack(pbs_p), jnp.stack(pbs_s), mk5, mv5)
```

```python
import functools
import math

import jax
import jax.numpy as jnp
from jax import lax
from jax.experimental import pallas as pl
from jax.experimental.pallas import tpu as pltpu

F32 = jnp.float32
BF16 = jnp.bfloat16
NT_DIMS = (((1,), (1,)), ((), ()))
TN_DIMS = (((0,), (0,)), ((), ()))

PAST_LEN = 16384
SSD_CHUNK = 128
POOL_WINDOWS = (2, 4, 8, 16)
POOL_CTX = 15
LN_EPS = 1e-5
RMS_EPS = 1e-5

V7X_LANES = 128
V7X_SUBLANES = 8
V7X_VMEM_BYTES = 64 * 1024 * 1024
VMEM_LIMIT = V7X_VMEM_BYTES - 8 * 1024 * 1024


def _tile(dim, pref, quantum):
    best = None
    t = quantum
    while t <= min(dim, pref):
        if dim % t == 0:
            best = t
        t += quantum
    return best if best is not None else dim


def _params(*sem):
    return pltpu.CompilerParams(dimension_semantics=sem, vmem_limit_bytes=VMEM_LIMIT)


def _silu(x):
    return x * jax.nn.sigmoid(x)


def _softplus(x):
    return jnp.maximum(x, 0.0) + jnp.log1p(jnp.exp(-jnp.abs(x)))


def _layer_norm(y, g, b):
    mu = jnp.mean(y, -1, keepdims=True)
    d = y - mu
    var = jnp.mean(d * d, -1, keepdims=True)
    return d * lax.rsqrt(var + LN_EPS) * g + b


def _split3(x):
    hi = x.astype(BF16)
    r = x - hi.astype(F32)
    mid = r.astype(BF16)
    lo = (r - mid.astype(F32)).astype(BF16)
    return hi, mid, lo


def _dot_sel(x, sel):
    hi, mid, lo = _split3(x)
    return (jnp.dot(hi, sel, preferred_element_type=F32) + jnp.dot(mid, sel, preferred_element_type=F32)
            + jnp.dot(lo, sel, preferred_element_type=F32))


def _sel_dot(sel, x):
    hi, mid, lo = _split3(x)
    return (jnp.dot(sel, hi, preferred_element_type=F32) + jnp.dot(sel, mid, preferred_element_type=F32)
            + jnp.dot(sel, lo, preferred_element_type=F32))


def _mm_kernel(x_ref, w_ref, o_ref, *scratch, act):
    if scratch:
        wb_ref, = scratch

        @pl.when(pl.program_id(1) == 0)
        def _():
            wb_ref[...] = w_ref[...].astype(BF16)

        w = wb_ref[...]
    else:
        w = w_ref[...]
    acc = jnp.dot(x_ref[...], w, preferred_element_type=F32)
    if act == "relu2":
        acc = jnp.square(jnp.maximum(acc, 0.0))
    o_ref[...] = acc.astype(o_ref.dtype)


def matmul(x, w, layer, out_dtype, act=None, bm=1024, bn=1024):
    M, K = x.shape
    N = w.shape[2]
    bm = _tile(M, bm, V7X_SUBLANES)
    bn = _tile(N, bn, V7X_LANES)
    scratch = [pltpu.VMEM((K, bn), BF16)] if w.dtype != BF16 else []
    return pl.pallas_call(
        functools.partial(_mm_kernel, act=act),
        grid=(N // bn, M // bm),
        in_specs=[pl.BlockSpec((bm, K), lambda j, i: (i, 0)),
                  pl.BlockSpec((None, K, bn), lambda j, i: (layer, 0, j))],
        out_specs=pl.BlockSpec((bm, bn), lambda j, i: (i, j)),
        out_shape=jax.ShapeDtypeStruct((M, N), out_dtype),
        scratch_shapes=scratch,
        compiler_params=_params("parallel", "arbitrary"),
    )(x, w)


def _kv_proj_kernel(x_ref, w_ref, o2_ref, o5_ref, wb_ref, *, heads):
    @pl.when(pl.program_id(1) == 0)
    def _():
        wb_ref[...] = w_ref[...].astype(BF16)

    acc = jnp.dot(x_ref[...], wb_ref[...], preferred_element_type=F32)
    o2_ref[...] = acc
    hd = acc.shape[1] // heads
    for h in range(heads):
        o5_ref[0, :, h, :] = acc[:, h * hd:(h + 1) * hd]


def kv_proj(mem_b, w, nb, heads):
    M, K = mem_b.shape
    depth, _, N = w.shape
    mem_len = M // nb
    hd = N // heads
    return pl.pallas_call(
        functools.partial(_kv_proj_kernel, heads=heads),
        grid=(depth, nb),
        in_specs=[pl.BlockSpec((mem_len, K), lambda l, b: (b, 0)),
                  pl.BlockSpec((None, K, N), lambda l, b: (l, 0, 0), pipeline_mode=pl.Buffered(1))],
        out_specs=[pl.BlockSpec((None, mem_len, N), lambda l, b: (l, b, 0)),
                   pl.BlockSpec((None, 1, mem_len, heads, hd), lambda l, b: (l, b, 0, 0, 0))],
        out_shape=[jax.ShapeDtypeStruct((depth, M, N), F32),
                   jax.ShapeDtypeStruct((depth, nb, mem_len, heads, hd), F32)],
        scratch_shapes=[pltpu.VMEM((K, N), BF16)],
        compiler_params=_params("parallel", "arbitrary"),
    )(mem_b, w)


def _mm_ln_kernel(*refs, alpha, n_first):
    if n_first is None:
        x_ref, w_ref, res_ref, g_ref, b_ref, of_ref, ob_ref = refs
    else:
        x_ref, x2_ref, w_ref, res_ref, g_ref, b_ref, of_ref, ob_ref = refs

    def body(x):
        m = jnp.dot(x, w_ref[...], preferred_element_type=F32)
        out = _layer_norm(alpha * res_ref[...] + m, g_ref[...], b_ref[...])
        of_ref[...] = out
        ob_ref[...] = out.astype(BF16)

    if n_first is None:
        body(x_ref[...])
    else:
        i = pl.program_id(0)

        @pl.when(i < n_first)
        def _():
            body(x_ref[...])

        @pl.when(i >= n_first)
        def _():
            body(x2_ref[...])


def matmul_ln(x, w, layer, res, g, b, alpha, x2=None):
    M1, K = x.shape
    M2 = 0 if x2 is None else x2.shape[0]
    M = M1 + M2
    N = w.shape[2]
    n_src = 1 if x2 is None else 2
    w_bytes = K * N * 2
    per_row = n_src * 2 * (K * 2) + 2 * (N * 4) + 2 * (N * 4) + 2 * (N * 2) + 2 * (N * 4)
    bm = _tile(math.gcd(M1, M2) if M2 else M1, min(512, (VMEM_LIMIT - w_bytes) // per_row), 2 * V7X_SUBLANES)
    n_first = M1 // bm
    x_specs = [pl.BlockSpec((bm, K), lambda i: (i, 0))]
    if x2 is not None:
        x_specs = [pl.BlockSpec((bm, K), lambda i: (jnp.minimum(i, n_first - 1), 0)),
                   pl.BlockSpec((bm, K), lambda i: (jnp.maximum(i - n_first, 0), 0))]
    return pl.pallas_call(
        functools.partial(_mm_ln_kernel, alpha=alpha, n_first=None if x2 is None else n_first),
        grid=(M // bm,),
        in_specs=x_specs + [
            pl.BlockSpec((None, K, N), lambda i: (layer, 0, 0), pipeline_mode=pl.Buffered(1)),
            pl.BlockSpec((bm, N), lambda i: (i, 0)),
            pl.BlockSpec((1, N), lambda i: (0, 0)),
            pl.BlockSpec((1, N), lambda i: (0, 0))],
        out_specs=[pl.BlockSpec((bm, N), lambda i: (i, 0)),
                   pl.BlockSpec((bm, N), lambda i: (i, 0))],
        out_shape=[jax.ShapeDtypeStruct((M, N), F32), jax.ShapeDtypeStruct((M, N), BF16)],
        compiler_params=_params("arbitrary"),
    )(*([x] if x2 is None else [x, x2]), w, res, g, b)


def _prompt_mixer_kernel(z_ref, xs_ref, bc_ref, gb_ref, gc_ref, hh_ref, dt_ref, convw_ref, convb_ref, dtb_ref,
                         alog_ref, dskip_ref, normg_ref, scw_ref, tril_ref, expand_ref,
                         y_ref, hout_ref, utail_ref, ctail_ref, state_ref, ext_ref, act_ref, extu_ref,
                         *, L, nc, G, R, P, N):
    RP = R * P
    D = G * RP
    GN = G * N
    CH = D + 2 * GN
    KC = convw_ref.shape[0]
    KS = scw_ref.shape[0]
    c = pl.program_id(1)

    @pl.when(c == 0)
    def _init():
        ext_ref[0:8, :] = jnp.zeros((8, CH), F32)
        extu_ref[0:8, :] = jnp.zeros((8, D), F32)
        state_ref[...] = jnp.zeros(state_ref.shape, F32)

    ext_ref[8:8 + L, 0:D] = xs_ref[...]
    ext_ref[8:8 + L, D:CH] = bc_ref[...]
    cw = min(512, CH)
    for j in range(CH // cw):
        cols = slice(j * cw, (j + 1) * cw)
        base = 8 - (KC - 1)
        acc = ext_ref[base:base + L, cols] * convw_ref[0:1, cols]
        for k in range(1, KC):
            acc = acc + ext_ref[base + k:base + k + L, cols] * convw_ref[k:k + 1, cols]
        act_ref[:, cols] = _silu(acc + convb_ref[:, cols])

    dt = _softplus(dt_ref[...] + dtb_ref[...])
    a = dt * (-jnp.exp(alog_ref[...]))
    acs = _sel_dot(tril_ref[...], a)
    dt_e = _dot_sel(dt, expand_ref[...])
    acs_e = _dot_sel(acs, expand_ref[...])
    acs_row = acs.T

    li = lax.broadcasted_iota(jnp.int32, (L, L), 0)
    si = lax.broadcasted_iota(jnp.int32, (L, L), 1)
    causal = si <= li
    ci = lax.broadcasted_iota(jnp.int32, (L, RP), 1)
    head_cols = [jnp.logical_and(ci >= r * P, ci < (r + 1) * P) for r in range(R)]

    for g in range(G):
        gc_ = slice(g * RP, (g + 1) * RP)
        xs_g = act_ref[:, gc_]
        b_g = act_ref[:, D + g * N:D + (g + 1) * N]
        c_g = act_ref[:, D + GN + g * N:D + GN + (g + 1) * N].astype(BF16)
        xdt = xs_g * dt_e[:, gc_]
        x_bd = jnp.concatenate([jnp.where(head_cols[r], xdt, 0.0) for r in range(R)], axis=0).astype(BF16)
        acs_g = acs_e[:, gc_]
        bt_g = b_g.T
        cb = jnp.dot(c_g, bt_g.astype(BF16), preferred_element_type=F32)

        ms, dbs = [], []
        for r in range(R):
            h = g * R + r
            col = acs[:, h:h + 1]
            row = acs_row[h:h + 1, :]
            ms.append(cb * jnp.exp(jnp.where(causal, col - row, -jnp.inf)))
            dbs.append(bt_g * jnp.exp(acs_row[h:h + 1, L - 1:L] - row))

        y = jnp.dot(jnp.concatenate(ms, axis=1).astype(BF16), x_bd, preferred_element_type=F32)
        h_prev = state_ref[g]
        y_off = jnp.dot(c_g, h_prev.astype(BF16), preferred_element_type=F32)
        dbt = jnp.concatenate(dbs, axis=1).astype(BF16)
        new_states = jnp.dot(dbt, x_bd, preferred_element_type=F32)
        state_ref[g] = h_prev * jnp.exp(acs_g[L - 1:L, :]) + new_states
        y = y + y_off * jnp.exp(acs_g)
        y = y + dskip_ref[:, gc_] * xs_g
        y = y * _silu(z_ref[:, gc_])
        y = y * lax.rsqrt(jnp.mean(y * y, -1, keepdims=True) + RMS_EPS)
        y_ref[:, gc_] = (y * normg_ref[:, gc_]).astype(BF16)

    cw = min(512, D)
    for j in range(D // cw):
        cols = slice(j * cw, (j + 1) * cw)
        extu_ref[8:8 + L, cols] = gc_ref[:, cols] * hh_ref[:, cols]
        base = 8 - (KS - 1)
        v = extu_ref[base:base + L, cols] * scw_ref[0:1, cols]
        for k in range(1, KS):
            v = v + extu_ref[base + k:base + k + L, cols] * scw_ref[k:k + 1, cols]
        y_ref[:, D + j * cw:D + (j + 1) * cw] = (gb_ref[:, cols] * v).astype(BF16)

    @pl.when(c == nc - 1)
    def _fin():
        utail_ref[0] = extu_ref[L:L + 8, :]
        ctail_ref[0] = ext_ref[L:L + 8, :]
        for g in range(G):
            hout_ref[0, g * R:(g + 1) * R] = state_ref[g].T.reshape(R, P, N)

    ext_ref[0:8, :] = ext_ref[L:L + 8, :]
    extu_ref[0:8, :] = extu_ref[L:L + 8, :]


def _mixer_consts(D, P):
    expand = (jnp.arange(V7X_LANES)[:, None] == (jnp.arange(D) // P)[None, :]).astype(BF16)
    return expand


def _proj_specs(rows, D, row_block):
    return [pl.BlockSpec((rows, D), functools.partial(lambda j, *idx: (row_block(*idx), j), j)) for j in (0, 1, 2)] * 2


def prompt_mixer(proj_a, proj_b, dt_raw, nb, T, cfg, wts):
    D, G, R, P, N = cfg["D"], cfg["G"], cfg["R"], cfg["P"], cfg["N"]
    H = G * R
    GN = G * N
    CH = D + 2 * GN
    L = SSD_CHUNK
    nc = T // L
    assert T % L == 0 and L == V7X_LANES and 2 * GN == D and R * P * G == D
    tril = jnp.tril(jnp.ones((L, L), BF16))
    expand = _mixer_consts(D, P)
    consts = list(wts) + [tril, expand]

    def row_block(b, c):
        return b * nc + c

    in_specs = (_proj_specs(L, D, row_block) + [pl.BlockSpec((L, V7X_LANES), lambda b, c: (b * nc + c, 0))]
                + [pl.BlockSpec(a.shape, functools.partial(lambda n, b, c: (0,) * n, a.ndim)) for a in consts])
    return pl.pallas_call(
        functools.partial(_prompt_mixer_kernel, L=L, nc=nc, G=G, R=R, P=P, N=N),
        grid=(nb, nc),
        in_specs=in_specs,
        out_specs=[pl.BlockSpec((L, 2 * D), lambda b, c: (b * nc + c, 0)),
                   pl.BlockSpec((1, H, P, N), lambda b, c: (b, 0, 0, 0)),
                   pl.BlockSpec((1, 8, D), lambda b, c: (b, 0, 0)),
                   pl.BlockSpec((1, 8, CH), lambda b, c: (b, 0, 0))],
        out_shape=[jax.ShapeDtypeStruct((nb * T, 2 * D), BF16),
                   jax.ShapeDtypeStruct((nb, H, P, N), F32),
                   jax.ShapeDtypeStruct((nb, 8, D), F32),
                   jax.ShapeDtypeStruct((nb, 8, CH), F32)],
        scratch_shapes=[pltpu.VMEM((G, N, R * P), F32), pltpu.VMEM((L + 8, CH), F32),
                        pltpu.VMEM((L, CH), F32), pltpu.VMEM((L + 8, D), F32)],
        compiler_params=_params("parallel", "arbitrary"),
    )(*([proj_a] * 3 + [proj_b] * 3 + [dt_raw] + consts))


def _sample_mixer_kernel(*refs, bb, T, G, R, P, N, n_prev):
    (z_ref, xs_ref, bc_ref, gb_ref, gc_ref, hh_ref, dt_ref, convw_ref, convb_ref, dtb_ref, alog_ref,
     dskip_ref, normg_ref, scw_ref, btril_ref, expand_ref, expand_t_ref, rep_ref,
     h0_ref, ctx_ref, scctx_ref) = refs[:21]
    prev_refs = refs[21:21 + n_prev]
    y_ref, hout_ref, utail_ref, ctail_ref, ext_ref, act_ref, extu_ref = refs[21 + n_prev:]
    H = G * R
    RP = R * P
    D = G * RP
    GN = G * N
    CH = D + 2 * GN
    HT = H * T
    KC = convw_ref.shape[0]
    KS = scw_ref.shape[0]
    rows = bb * T

    ext_ref[:, 8 - (KC - 1):8, :] = ctx_ref[...]
    ext_ref[:, 8:8 + T, 0:D] = xs_ref[...].reshape(bb, T, D)
    ext_ref[:, 8:8 + T, D:CH] = bc_ref[...].reshape(bb, T, 2 * GN)
    ctail_ref[...] = ext_ref[:, T:T + 8, :]
    cw = min(512, CH)
    for j in range(CH // cw):
        cols = slice(j * cw, (j + 1) * cw)
        base = 8 - (KC - 1)
        acc = ext_ref[:, base:base + T, cols] * convw_ref[0:1, cols]
        for k in range(1, KC):
            acc = acc + ext_ref[:, base + k:base + k + T, cols] * convw_ref[k:k + 1, cols]
        act_ref[:, cols] = _silu(acc + convb_ref[:, cols]).reshape(rows, cw)

    dt = _softplus(dt_ref[...] + dtb_ref[...])
    a = dt * (-jnp.exp(alog_ref[...]))
    acs = _sel_dot(btril_ref[...], a)
    dt_e = _dot_sel(dt, expand_ref[...])
    acs_e = _dot_sel(acs, expand_ref[...])
    acs_e3 = acs_e.reshape(bb, T, D)
    last_e = acs_e3[:, T - 1:T, :]
    xs_act = act_ref[:, 0:D]
    xdt = xs_act * dt_e
    xde = xdt * jnp.exp(last_e - acs_e3).reshape(rows, D)
    a1 = _dot_sel(acs, expand_t_ref[...]).reshape(bb, T, HT)
    l3 = lax.broadcasted_iota(jnp.int32, (bb, T, HT), 1)
    s3 = lax.broadcasted_iota(jnp.int32, (bb, T, HT), 2) & (T - 1)
    a2 = jnp.sum(jnp.where(l3 == s3, a1, 0.0), axis=1, keepdims=True)
    decay = jnp.exp(jnp.where(s3 <= l3, a1 - a2, -jnp.inf)).reshape(rows, HT)
    cd_heads = jnp.exp(acs)

    ri = lax.broadcasted_iota(jnp.int32, (HT, D), 0)
    ci = lax.broadcasted_iota(jnp.int32, (HT, D), 1)
    bd_mask = (ri >> (T.bit_length() - 1)) == (ci >> (P.bit_length() - 1))
    lane_g = lax.broadcasted_iota(jnp.int32, (T, HT), 1) >> ((R * T).bit_length() - 1)

    y_diag, y_off = [], []
    for b in range(bb):
        rs = slice(b * T, (b + 1) * T)
        b_pieces = [act_ref[rs, D + g * N:D + (g + 1) * N] for g in range(G)]
        c_st = jnp.concatenate([act_ref[rs, D + GN + g * N:D + GN + (g + 1) * N] for g in range(G)],
                               axis=0).astype(BF16)
        b_st = jnp.concatenate(b_pieces, axis=0).astype(BF16)
        call = lax.dot_general(c_st, b_st, NT_DIMS, preferred_element_type=F32)
        q = _dot_sel(call, rep_ref[...])
        cb = jnp.zeros((T, HT), F32)
        for g in range(G):
            cb = cb + jnp.where(lane_g == g, q[g * T:(g + 1) * T], 0.0)
        mcat = (cb * decay[rs]).astype(BF16)
        x_bd = jnp.where(bd_mask, jnp.concatenate([xdt[rs]] * H, axis=0), 0.0).astype(BF16)
        y_diag.append(jnp.dot(mcat, x_bd, preferred_element_type=F32))

        h0_b = h0_ref[b].reshape(D, N)
        yy = lax.dot_general(c_st, h0_b.astype(BF16), NT_DIMS, preferred_element_type=F32)
        y_off.append(jnp.concatenate([yy[g * T:(g + 1) * T, g * RP:(g + 1) * RP] for g in range(G)], axis=1))

        x_bd2 = jnp.where(bd_mask, jnp.concatenate([xde[rs]] * H, axis=0), 0.0).astype(BF16)
        b_rep = jnp.concatenate([b_pieces[h // R] for h in range(H)], axis=0).astype(BF16)
        new_states = lax.dot_general(x_bd2, b_rep, TN_DIMS, preferred_element_type=F32)
        r_last = b * T + T - 1
        cd = jnp.concatenate([jnp.broadcast_to(cd_heads[r_last:r_last + 1, h:h + 1], (P, N)) for h in range(H)],
                             axis=0)
        hout_ref[n_prev, b] = (h0_b * cd + new_states).reshape(H, P, N)
    for s, prev_ref in enumerate(prev_refs):
        hout_ref[s] = prev_ref[...]

    y = jnp.concatenate(y_diag, axis=0) + jnp.concatenate(y_off, axis=0) * jnp.exp(acs_e)
    y = y + dskip_ref[...] * xs_act
    y = y * _silu(z_ref[...])
    for g in range(G):
        gc_ = slice(g * RP, (g + 1) * RP)
        yg = y[:, gc_]
        yg = yg * lax.rsqrt(jnp.mean(yg * yg, -1, keepdims=True) + RMS_EPS)
        y_ref[:, gc_] = (yg * normg_ref[:, gc_]).astype(BF16)

    extu_ref[:, 8 - (KS - 1):8, :] = scctx_ref[...]
    cw = min(512, D)
    for j in range(D // cw):
        cols = slice(j * cw, (j + 1) * cw)
        extu_ref[:, 8:8 + T, cols] = (gc_ref[:, cols] * hh_ref[:, cols]).reshape(bb, T, cw)
        base = 8 - (KS - 1)
        v = extu_ref[:, base:base + T, cols] * scw_ref[0:1, cols]
        for k in range(1, KS):
            v = v + extu_ref[:, base + k:base + k + T, cols] * scw_ref[k:k + 1, cols]
        y_ref[:, D + j * cw:D + (j + 1) * cw] = (gb_ref[:, cols] * v.reshape(rows, cw)).astype(BF16)
    utail_ref[...] = extu_ref[:, T:T + 8, :]


def sample_mixer(proj_a, proj_b, dt_raw, row0, nb, T, cfg, wts, state_ssd, state_ssd_conv, state_short_conv,
                 layer_e, prev_states=(), bb=4):
    D, G, R, P, N = cfg["D"], cfg["G"], cfg["R"], cfg["P"], cfg["N"]
    H = G * R
    GN = G * N
    CH = D + 2 * GN
    bb = _tile(nb, bb, 2)
    rows = bb * T
    pow2 = lambda v: v & (v - 1) == 0
    assert T == V7X_SUBLANES and row0 % rows == 0 and 2 * GN == D and R * P * G == D
    assert pow2(P) and pow2(R)
    rb0 = row0 // rows
    kc1, ks1 = state_ssd_conv.shape[2], state_short_conv.shape[2]
    assert kc1 <= T and ks1 <= T
    seq = jnp.arange(rows) // T
    btril = jnp.logical_and(seq[:, None] == seq[None, :], jnp.arange(rows)[:, None] >= jnp.arange(rows)[None, :])
    expand = _mixer_consts(D, P)
    lane_h = jnp.arange(H * T) // T
    expand_t = (jnp.arange(V7X_LANES)[:, None] == lane_h[None, :]).astype(BF16)
    rep = jnp.logical_and((jnp.arange(G * T) // T)[:, None] == (lane_h // R)[None, :],
                          (jnp.arange(G * T) % T)[:, None] == (jnp.arange(H * T) % T)[None, :]).astype(BF16)
    consts = list(wts) + [btril.astype(BF16), expand, expand_t, rep]
    n_prev = len(prev_states)

    in_specs = (_proj_specs(rows, D, lambda i: rb0 + i) + [pl.BlockSpec((rows, V7X_LANES), lambda i: (rb0 + i, 0))]
                + [pl.BlockSpec(a.shape, functools.partial(lambda n, i: (0,) * n, a.ndim)) for a in consts]
                + [pl.BlockSpec((None, bb, H, P, N), lambda i: (layer_e, i, 0, 0, 0)),
                   pl.BlockSpec((None, bb, kc1, CH), lambda i: (layer_e, i, 0, 0)),
                   pl.BlockSpec((None, bb, ks1, D), lambda i: (layer_e, i, 0, 0))]
                + [pl.BlockSpec((bb, H, P, N), lambda i: (i, 0, 0, 0))] * n_prev)
    return pl.pallas_call(
        functools.partial(_sample_mixer_kernel, bb=bb, T=T, G=G, R=R, P=P, N=N, n_prev=n_prev),
        grid=(nb // bb,),
        in_specs=in_specs,
        out_specs=[pl.BlockSpec((rows, 2 * D), lambda i: (i, 0)),
                   pl.BlockSpec((n_prev + 1, bb, H, P, N), lambda i: (0, i, 0, 0, 0)),
                   pl.BlockSpec((bb, 8, D), lambda i: (i, 0, 0)),
                   pl.BlockSpec((bb, 8, CH), lambda i: (i, 0, 0))],
        out_shape=[jax.ShapeDtypeStruct((nb * T, 2 * D), BF16),
                   jax.ShapeDtypeStruct((n_prev + 1, nb, H, P, N), F32),
                   jax.ShapeDtypeStruct((nb, 8, D), F32),
                   jax.ShapeDtypeStruct((nb, 8, CH), F32)],
        scratch_shapes=[pltpu.VMEM((bb, 8 + T, CH), F32), pltpu.VMEM((rows, CH), F32),
                        pltpu.VMEM((bb, 8 + T, D), F32)],
        compiler_params=_params("parallel"),
    )(*([proj_a] * 3 + [proj_b] * 3 + [dt_raw] + consts + [state_ssd, state_ssd_conv, state_short_conv]
        + list(prev_states)))


def _pool_sample_kernel(st_ref, x_ref, p_ref, ext_ref, *, start_pos, dg):
    T = x_ref.shape[1]
    ext_ref[:, 16 - POOL_CTX:16, :] = st_ref[...]
    ext_ref[:, 16:16 + T, :] = x_ref[...]
    pos = start_pos + lax.broadcasted_iota(jnp.int32, (1, T, 1), 1)
    for gi, w in enumerate(POOL_WINDOWS):
        cols = slice(gi * dg, (gi + 1) * dg)
        s = ext_ref[:, 16:16 + T, cols]
        for j in range(1, w):
            s = s + ext_ref[:, 16 - j:16 - j + T, cols]
        cnt = jnp.minimum(pos + 1, w).astype(F32)
        p_ref[:, :, cols] = (s / cnt - x_ref[:, :, cols]).astype(BF16)


def pool_sample(state_pool, layer_o, x3, dg):
    B, T, D = x3.shape
    bb = _tile(B, 16, 1)
    return pl.pallas_call(
        functools.partial(_pool_sample_kernel, start_pos=PAST_LEN, dg=dg),
        grid=(B // bb,),
        in_specs=[pl.BlockSpec((None, bb, POOL_CTX, D), lambda i: (layer_o, i, 0, 0)),
                  pl.BlockSpec((bb, T, D), lambda i: (i, 0, 0))],
        out_specs=pl.BlockSpec((bb, T, D), lambda i: (i, 0, 0)),
        out_shape=jax.ShapeDtypeStruct((B, T, D), BF16),
        scratch_shapes=[pltpu.VMEM((bb, 16 + T, D), F32)],
        compiler_params=_params("parallel"),
    )(state_pool, x3)


def _pool_ln_kernel(x_ref, halo_ref, ps_ref, w_ref, scale_ref, g_ref, b_ref, of_ref, ob_ref, ext_ref, p_ref,
                    *, n_prompt_tiles, tiles_per_seq, dg, alpha):
    i = pl.program_id(0)
    bm = x_ref.shape[0]

    @pl.when(i < n_prompt_tiles)
    def _prompt():
        first = (i % tiles_per_seq) == 0
        ext_ref[0:16, :] = jnp.where(first, 0.0, halo_ref[...])
        ext_ref[16:16 + bm, :] = x_ref[...]
        pos = (i % tiles_per_seq) * bm + lax.broadcasted_iota(jnp.int32, (bm, 1), 0)
        for gi, w in enumerate(POOL_WINDOWS):
            cols = slice(gi * dg, (gi + 1) * dg)
            s = ext_ref[16:16 + bm, cols]
            for j in range(1, w):
                s = s + ext_ref[16 - j:16 - j + bm, cols]
            cnt = jnp.minimum(pos + 1, w).astype(F32)
            p_ref[:, cols] = (s / cnt - x_ref[:, cols]).astype(BF16)

    @pl.when(i >= n_prompt_tiles)
    def _sample():
        p_ref[...] = ps_ref[...]

    for gi in range(len(POOL_WINDOWS)):
        cols = slice(gi * dg, (gi + 1) * dg)
        m = jnp.dot(p_ref[:, cols], w_ref[gi], preferred_element_type=F32) * scale_ref[:, cols]
        ext_ref[16:16 + bm, cols] = alpha * x_ref[:, cols] + m
    out = _layer_norm(ext_ref[16:16 + bm, :], g_ref[...], b_ref[...])
    of_ref[...] = out
    ob_ref[...] = out.astype(BF16)


def pool_ln(x, p_sample, w_pool, scale, g, b, n_prompt_rows, T, alpha, bm=512):
    M, D = x.shape
    dg = D // len(POOL_WINDOWS)
    bm = _tile(math.gcd(math.gcd(n_prompt_rows, M - n_prompt_rows), T), bm, 16)
    npt = n_prompt_rows // bm
    hb = bm // 16
    return pl.pallas_call(
        functools.partial(_pool_ln_kernel, n_prompt_tiles=npt, tiles_per_seq=T // bm, dg=dg, alpha=alpha),
        grid=(M // bm,),
        in_specs=[pl.BlockSpec((bm, D), lambda i: (i, 0)),
                  pl.BlockSpec((16, D), lambda i: (jnp.maximum(i * hb - 1, 0), 0)),
                  pl.BlockSpec((bm, D), lambda i: (jnp.maximum(i - npt, 0), 0)),
                  pl.BlockSpec(w_pool.shape, lambda i: (0, 0, 0)),
                  pl.BlockSpec((1, D), lambda i: (0, 0)),
                  pl.BlockSpec((1, D), lambda i: (0, 0)),
                  pl.BlockSpec((1, D), lambda i: (0, 0))],
        out_specs=[pl.BlockSpec((bm, D), lambda i: (i, 0)), pl.BlockSpec((bm, D), lambda i: (i, 0))],
        out_shape=[jax.ShapeDtypeStruct((M, D), F32), jax.ShapeDtypeStruct((M, D), BF16)],
        scratch_shapes=[pltpu.VMEM((bm + 16, D), F32), pltpu.VMEM((bm, D), BF16)],
        compiler_params=_params("arbitrary"),
    )(x, x, p_sample, w_pool, scale, g, b)


def _softmax_rows(s):
    m = jnp.max(s, -1, keepdims=True)
    e = jnp.exp(s - m)
    return e / jnp.sum(e, -1, keepdims=True)


def _attn_prompt_kernel(q_ref, k_ref, v_ref, o_ref, *, scale):
    s = lax.dot_general(q_ref[...], k_ref[...].astype(BF16), (((1,), (1,)), ((), ())),
                        preferred_element_type=F32) * scale
    p = _softmax_rows(s).astype(BF16)
    o_ref[...] = jnp.dot(p, v_ref[...].astype(BF16), preferred_element_type=F32).astype(BF16)


def attn_prompt(q, mk, mv, layer, nb, T, mem_len, heads, tq=1024):
    M, D = q.shape
    hd = D // heads
    tq = _tile(T, tq, V7X_SUBLANES)
    nt = T // tq
    return pl.pallas_call(
        functools.partial(_attn_prompt_kernel, scale=hd ** -0.5),
        grid=(nb, heads, nt),
        in_specs=[pl.BlockSpec((tq, hd), lambda b, h, t: (b * nt + t, h)),
                  pl.BlockSpec((None, mem_len, hd), lambda b, h, t: (layer, b, h)),
                  pl.BlockSpec((None, mem_len, hd), lambda b, h, t: (layer, b, h))],
        out_specs=pl.BlockSpec((tq, hd), lambda b, h, t: (b * nt + t, h)),
        out_shape=jax.ShapeDtypeStruct((M, D), BF16),
        input_output_aliases={0: 0},
        compiler_params=_params("parallel", "parallel", "parallel"),
    )(q, mk, mv)


def _attn_sample_kernel(q_ref, k_hbm, v_hbm, o_ref, kbuf, vbuf, sem, *, layer, scale, T):
    _, heads, bb, mem_len, hd = kbuf.shape
    i = pl.program_id(0)
    n = pl.num_programs(0)

    def copies(step, slot):
        cps = []
        for h in range(heads):
            for t, (src, dst) in enumerate(((k_hbm, kbuf), (v_hbm, vbuf))):
                cps.append(pltpu.make_async_copy(src.at[layer, pl.ds(step * bb, bb), :, h, :],
                                                 dst.at[slot, h], sem.at[slot, t, h]))
        return cps

    @pl.when(i == 0)
    def _():
        for cp in copies(0, 0):
            cp.start()

    slot = i % 2

    @pl.when(i + 1 < n)
    def _():
        for cp in copies(i + 1, 1 - slot):
            cp.start()

    for cp in copies(i, slot):
        cp.wait()

    slots = bb * heads
    q_all = q_ref[...].astype(F32)
    st = jnp.zeros((mem_len, slots * T), F32)
    for b in range(bb):
        for h in range(heads):
            s_ = b * heads + h
            q_bh = q_all[b * T:(b + 1) * T, h * hd:(h + 1) * hd]
            parts = []
            if s_:
                parts.append(jnp.zeros((s_ * T, hd), F32))
            parts.append(q_bh)
            if s_ + 1 < slots:
                parts.append(jnp.zeros(((slots - s_ - 1) * T, hd), F32))
            q_slot = jnp.concatenate(parts, axis=0).astype(BF16)
            st = st + lax.dot_general(kbuf[slot, h, b].astype(BF16), q_slot, (((1,), (1,)), ((), ())),
                                      preferred_element_type=F32)
    st = st * scale
    e = jnp.exp(st - jnp.max(st, 0, keepdims=True))
    p_all = (e / jnp.sum(e, 0, keepdims=True)).T
    rows = []
    for b in range(bb):
        outs = []
        for h in range(heads):
            s_ = b * heads + h
            p = p_all[s_ * T:(s_ + 1) * T, :].astype(BF16)
            outs.append(jnp.dot(p, vbuf[slot, h, b].astype(BF16), preferred_element_type=F32))
        rows.append(jnp.concatenate(outs, axis=1))
    o_ref[...] = jnp.concatenate(rows, axis=0).astype(BF16)


def attn_sample(q, cache_k, cache_v, layer, row0, T):
    _, B, mem_len, heads, hd = cache_k.shape
    M, D = q.shape
    bb = V7X_LANES // (heads * T)
    assert bb * heads * T == V7X_LANES and B % bb == 0
    rows = bb * T
    assert row0 % rows == 0 and rows % 16 == 0
    rb0 = row0 // rows
    return pl.pallas_call(
        functools.partial(_attn_sample_kernel, layer=layer, scale=hd ** -0.5, T=T),
        grid=(B // bb,),
        in_specs=[pl.BlockSpec((rows, D), lambda i: (rb0 + i, 0)),
                  pl.BlockSpec(memory_space=pl.ANY),
                  pl.BlockSpec(memory_space=pl.ANY)],
        out_specs=pl.BlockSpec((rows, D), lambda i: (rb0 + i, 0)),
        out_shape=jax.ShapeDtypeStruct((M, D), BF16),
        input_output_aliases={0: 0},
        scratch_shapes=[pltpu.VMEM((2, heads, bb, mem_len, hd), F32),
                        pltpu.VMEM((2, heads, bb, mem_len, hd), F32),
                        pltpu.SemaphoreType.DMA((2, 2, heads))],
        compiler_params=_params("arbitrary"),
    )(q, cache_k, cache_v)


def kernel(x_prompt, x_sample, state_ssd, state_ssd_conv, state_short_conv, state_pool, cache_mem_k, cache_mem_v, mem_prompt, w_in_even, ssd_conv_w, ssd_conv_b, ssd_dt_bias, ssd_a_log, ssd_d, ssd_norm_g, sc_conv_w, w_out_even, w_pool, pool_scale, wq_x, wk_x, wv_x, wo_x, w_up, w_down, ln_g, ln_b):
    BP, T, D = x_prompt.shape
    BS, TS, _ = x_sample.shape
    depth = wq_x.shape[0]
    n_even = w_in_even.shape[0]
    _, _, H, P, N = state_ssd.shape
    CH = state_ssd_conv.shape[-1]
    G = (CH - D) // (2 * N)
    R = H // G
    cfg = dict(D=D, G=G, R=R, P=P, N=N)
    mem_len, heads, hd = cache_mem_k.shape[2:]
    dg = w_pool.shape[-1]
    alpha = (2.0 * depth) ** 0.25
    MP, MS = BP * T, BS * TS
    o_dt = D + CH
    assert ssd_d.shape[-1] == H and H <= V7X_LANES and heads * hd == D

    xf = jnp.concatenate([x_prompt.reshape(MP, D), x_sample.reshape(MS, D)], axis=0)
    xb = xf.astype(BF16)
    mem_b = mem_prompt.reshape(BP * mem_len, D).astype(BF16)
    mk2, mk5 = kv_proj(mem_b, wk_x, BP, heads)
    mv2, mv5 = kv_proj(mem_b, wv_x, BP, heads)

    w_a = w_in_even[:, :, :o_dt].astype(BF16)
    w_b = w_in_even[:, :, o_dt + H:].astype(BF16)
    w_dt = jnp.pad(w_in_even[:, :, o_dt:o_dt + H], ((0, 0), (0, 0), (0, V7X_LANES - H))).astype(BF16)
    w_out_b = w_out_even.astype(BF16)
    wo_b = wo_x.astype(BF16)
    w_down_b = w_down.astype(BF16)
    w_pool_b = w_pool.astype(BF16)

    def pad_lanes(v):
        return jnp.pad(v.astype(F32), (0, V7X_LANES - v.shape[0]))[None, :]

    def last_rows(x2, nb, t, n):
        return jnp.stack([x2[(b + 1) * t - n:(b + 1) * t] for b in range(nb)])

    kc, ks = ssd_conv_w.shape[1], sc_conv_w.shape[1]
    hs_p, hs_s, cbs_p, cbs_s, sbs_p, sbs_s, pbs_p, pbs_s = [], [], [], [], [], [], [], []
    h_s = None
    for layer in range(depth):
        ln = lambda j: (ln_g[layer, j][None, :], ln_b[layer, j][None, :])
        if layer % 2 == 0:
            e = layer // 2
            proj_a = matmul(xb, w_a, e, F32)
            proj_b = matmul(xb, w_b, e, F32)
            dt_raw = matmul(xb, w_dt, e, F32)
            wts = (ssd_conv_w[e], ssd_conv_b[e][None, :], pad_lanes(ssd_dt_bias[e]), pad_lanes(ssd_a_log[e]),
                   jnp.repeat(ssd_d[e].astype(F32), P)[None, :], ssd_norm_g[e][None, :], sc_conv_w[e])
            y_p, h_p, ut_p, ct_p = prompt_mixer(proj_a, proj_b, dt_raw, BP, T, cfg, wts)
            y_s, h_s, ut_s, ct_s = sample_mixer(proj_a, proj_b, dt_raw, MP, BS, TS, cfg, wts,
                                                state_ssd, state_ssd_conv, state_short_conv, e,
                                                prev_states=hs_s if e == n_even - 1 else ())
            hs_s.append(h_s[0])
            hs_p.append(h_p)
            cbs_p.append(ct_p[:, 8 - (kc - 1):])
            cbs_s.append(ct_s[:, 8 - (kc - 1):])
            sbs_p.append(ut_p[:, 8 - (ks - 1):])
            sbs_s.append(ut_s[:, 8 - (ks - 1):])
            xf, xb = matmul_ln(y_p, w_out_b, e, xf, *ln(0), alpha, x2=y_s)
        else:
            o = layer // 2
            xs3 = xf[MP:].reshape(BS, TS, D)
            pbs_p.append(last_rows(xf, BP, T, POOL_CTX))
            pbs_s.append(jnp.concatenate([state_pool[o][:, TS:], xs3], axis=1))
            p_s = pool_sample(state_pool, o, xs3, dg).reshape(MS, D)
            xf, xb = pool_ln(xf, p_s, w_pool_b[o], pool_scale[o][None, :], *ln(0), MP, T, alpha)

        q = matmul(xb, wq_x, layer, BF16)
        att = attn_prompt(q, mk2, mv2, layer, BP, T, mem_len, heads)
        att = attn_sample(att, cache_mem_k, cache_mem_v, layer, MP, TS)
        xf, xb = matmul_ln(att, wo_b, layer, xf, *ln(1), alpha)

        hid = matmul(xb, w_up, layer, BF16, act="relu2")
        xf, xb = matmul_ln(hid, w_down_b, layer, xf, *ln(2), alpha)

    return (xf[:MP].reshape(BP, T, D), xf[MP:].reshape(BS, TS, D),
            jnp.stack(hs_p), h_s, jnp.stack(cbs_p), jnp.stack(cbs_s),
            jnp.stack(sbs_p), jnp.stack(sbs_s), jnp.stack(pbs_p), jnp.stack(pbs_s), mk5, mv5)
```

```python
import functools
import math

import jax
import jax.numpy as jnp
from jax import lax
from jax.experimental import pallas as pl
from jax.experimental.pallas import tpu as pltpu

F32 = jnp.float32
BF16 = jnp.bfloat16
NT_DIMS = (((1,), (1,)), ((), ()))
TN_DIMS = (((0,), (0,)), ((), ()))

PAST_LEN = 16384
SSD_CHUNK = 128
POOL_WINDOWS = (2, 4, 8, 16)
POOL_CTX = 15
LN_EPS = 1e-5
RMS_EPS = 1e-5

V7X_LANES = 128
V7X_SUBLANES = 8
V7X_VMEM_BYTES = 64 * 1024 * 1024
VMEM_LIMIT = V7X_VMEM_BYTES - 8 * 1024 * 1024


def _tile(dim, pref, quantum):
    best = None
    t = quantum
    while t <= min(dim, pref):
        if dim % t == 0:
            best = t
        t += quantum
    return best if best is not None else dim


def _params(*sem):
    return pltpu.CompilerParams(dimension_semantics=sem, vmem_limit_bytes=VMEM_LIMIT)


def _silu(x):
    return x * jax.nn.sigmoid(x)


def _softplus(x):
    return jnp.maximum(x, 0.0) + jnp.log1p(jnp.exp(-jnp.abs(x)))


def _layer_norm(y, g, b):
    mu = jnp.mean(y, -1, keepdims=True)
    d = y - mu
    var = jnp.mean(d * d, -1, keepdims=True)
    return d * lax.rsqrt(var + LN_EPS) * g + b


def _split3(x):
    hi = x.astype(BF16)
    r = x - hi.astype(F32)
    mid = r.astype(BF16)
    lo = (r - mid.astype(F32)).astype(BF16)
    return hi, mid, lo


def _dot_sel(x, sel):
    hi, mid, lo = _split3(x)
    return (jnp.dot(hi, sel, preferred_element_type=F32) + jnp.dot(mid, sel, preferred_element_type=F32)
            + jnp.dot(lo, sel, preferred_element_type=F32))


def _sel_dot(sel, x):
    hi, mid, lo = _split3(x)
    return (jnp.dot(sel, hi, preferred_element_type=F32) + jnp.dot(sel, mid, preferred_element_type=F32)
            + jnp.dot(sel, lo, preferred_element_type=F32))


def _mm_kernel(*refs, act, transposed, has_side):
    if has_side:
        x_ref, w_ref, side_ref, o_ref, side_out_ref = refs[:5]
        scratch = refs[5:]
        side_out_ref[...] = side_ref[...].astype(BF16)
    else:
        x_ref, w_ref, o_ref = refs[:3]
        scratch = refs[3:]
    if scratch:
        wb_ref, = scratch

        @pl.when(pl.program_id(1) == 0)
        def _():
            wb_ref[...] = (w_ref[0].T if transposed else w_ref[...]).astype(BF16)

        w = wb_ref[...]
    else:
        w = w_ref[...]
    acc = jnp.dot(x_ref[...], w, preferred_element_type=F32)
    if act == "relu2":
        acc = jnp.square(jnp.maximum(acc, 0.0))
    o_ref[...] = acc.astype(o_ref.dtype)


def matmul(x, w, layer, out_dtype, act=None, transposed=False, col0=0, n_cols=None, side=None, bm=1024, bn=1024):
    M, K = x.shape
    if transposed:
        assert w.dtype != BF16 and col0 % V7X_SUBLANES == 0
        N = n_cols
    else:
        assert col0 == 0 and n_cols is None
        N = w.shape[2]
    bm = _tile(M, bm, V7X_SUBLANES)
    bn = _tile(N, bn, V7X_LANES)
    n_i = M // bm
    if transposed:
        w_spec = pl.BlockSpec((pl.Element(1), pl.Element(bn), pl.Element(K)),
                              lambda j, i: (layer, pl.multiple_of(col0 + j * bn, V7X_SUBLANES), 0))
    else:
        w_spec = pl.BlockSpec((None, K, bn), lambda j, i: (layer, 0, j))
    in_specs = [pl.BlockSpec((bm, K), lambda j, i: (i, 0)), w_spec]
    out_specs = [pl.BlockSpec((bm, bn), lambda j, i: (i, j))]
    out_shape = [jax.ShapeDtypeStruct((M, N), out_dtype)]
    args = [x, w]
    if side is not None:
        s_arr, s_layer = side
        _, s_rows, s_cols = s_arr.shape
        n_steps = (N // bn) * n_i
        rb = next(r for r in range(16, s_rows + 1, 16) if s_rows % r == 0 and s_rows // r <= n_steps)
        last = s_rows // rb - 1
        in_specs.append(pl.BlockSpec((None, rb, s_cols), lambda j, i: (s_layer, jnp.minimum(j * n_i + i, last), 0)))
        out_specs.append(pl.BlockSpec((rb, s_cols), lambda j, i: (jnp.minimum(j * n_i + i, last), 0)))
        out_shape.append(jax.ShapeDtypeStruct((s_rows, s_cols), BF16))
        args.append(s_arr)
    scratch = [pltpu.VMEM((K, bn), BF16)] if w.dtype != BF16 else []
    outs = pl.pallas_call(
        functools.partial(_mm_kernel, act=act, transposed=transposed, has_side=side is not None),
        grid=(N // bn, n_i),
        in_specs=in_specs,
        out_specs=out_specs,
        out_shape=out_shape,
        scratch_shapes=scratch,
        compiler_params=_params("arbitrary", "arbitrary"),
    )(*args)
    return outs[0] if side is None else outs


def _kv_proj_kernel(x_ref, w_ref, o2_ref, o5_ref, wb_ref, *, heads):
    @pl.when(pl.program_id(1) == 0)
    def _():
        wb_ref[...] = w_ref[...].astype(BF16)

    acc = jnp.dot(x_ref[...], wb_ref[...], preferred_element_type=F32)
    o2_ref[...] = acc
    hd = acc.shape[1] // heads
    for h in range(heads):
        o5_ref[0, :, h, :] = acc[:, h * hd:(h + 1) * hd]


def kv_proj(mem_b, w, nb, heads):
    M, K = mem_b.shape
    depth, _, N = w.shape
    mem_len = M // nb
    hd = N // heads
    return pl.pallas_call(
        functools.partial(_kv_proj_kernel, heads=heads),
        grid=(depth, nb),
        in_specs=[pl.BlockSpec((mem_len, K), lambda l, b: (b, 0)),
                  pl.BlockSpec((None, K, N), lambda l, b: (l, 0, 0), pipeline_mode=pl.Buffered(1))],
        out_specs=[pl.BlockSpec((None, mem_len, N), lambda l, b: (l, b, 0)),
                   pl.BlockSpec((None, 1, mem_len, heads, hd), lambda l, b: (l, b, 0, 0, 0))],
        out_shape=[jax.ShapeDtypeStruct((depth, M, N), F32),
                   jax.ShapeDtypeStruct((depth, nb, mem_len, heads, hd), F32)],
        scratch_shapes=[pltpu.VMEM((K, N), BF16)],
        compiler_params=_params("parallel", "arbitrary"),
    )(mem_b, w)


def _mm_ln_kernel(*refs, alpha, n_first):
    if n_first is None:
        x_ref, w_ref, res_ref, g_ref, b_ref, of_ref, ob_ref = refs
    else:
        x_ref, x2_ref, w_ref, res_ref, g_ref, b_ref, of_ref, ob_ref = refs

    bm = res_ref.shape[0]
    n_chunks = 2 if bm >= 512 else 1
    ch = bm // n_chunks

    def body(src_ref):
        for c in range(n_chunks):
            rows = slice(c * ch, (c + 1) * ch)
            m = jnp.dot(src_ref[rows, :], w_ref[...], preferred_element_type=F32)
            out = _layer_norm(alpha * res_ref[rows, :] + m, g_ref[...], b_ref[...])
            of_ref[rows, :] = out
            ob_ref[rows, :] = out.astype(BF16)

    if n_first is None:
        body(x_ref)
    else:
        i = pl.program_id(0)

        @pl.when(i < n_first)
        def _():
            body(x_ref)

        @pl.when(i >= n_first)
        def _():
            body(x2_ref)


def matmul_ln(x, w, layer, res, g, b, alpha, x2=None):
    M1, K = x.shape
    M2 = 0 if x2 is None else x2.shape[0]
    M = M1 + M2
    N = w.shape[-1]
    n_src = 1 if x2 is None else 2
    w_bytes = K * N * 2
    per_row = n_src * 2 * (K * 2) + 2 * (N * 4) + 2 * (N * 4) + 2 * (N * 2) + 2 * (N * 4)
    bm = _tile(math.gcd(M1, M2) if M2 else M1, min(512, (VMEM_LIMIT - w_bytes) // per_row), 2 * V7X_SUBLANES)
    n_first = M1 // bm
    x_specs = [pl.BlockSpec((bm, K), lambda i: (i, 0))]
    if x2 is not None:
        x_specs = [pl.BlockSpec((bm, K), lambda i: (jnp.minimum(i, n_first - 1), 0)),
                   pl.BlockSpec((bm, K), lambda i: (jnp.maximum(i - n_first, 0), 0))]
    if layer is None:
        w_spec = pl.BlockSpec((K, N), lambda i: (0, 0), pipeline_mode=pl.Buffered(1))
    else:
        w_spec = pl.BlockSpec((None, K, N), lambda i: (layer, 0, 0), pipeline_mode=pl.Buffered(1))
    return pl.pallas_call(
        functools.partial(_mm_ln_kernel, alpha=alpha, n_first=None if x2 is None else n_first),
        grid=(M // bm,),
        in_specs=x_specs + [
            w_spec,
            pl.BlockSpec((bm, N), lambda i: (i, 0)),
            pl.BlockSpec((1, N), lambda i: (0, 0)),
            pl.BlockSpec((1, N), lambda i: (0, 0))],
        out_specs=[pl.BlockSpec((bm, N), lambda i: (i, 0)),
                   pl.BlockSpec((bm, N), lambda i: (i, 0))],
        out_shape=[jax.ShapeDtypeStruct((M, N), F32), jax.ShapeDtypeStruct((M, N), BF16)],
        compiler_params=_params("arbitrary"),
    )(*([x] if x2 is None else [x, x2]), w, res, g, b)


def _prompt_mixer_kernel(z_ref, xs_ref, bc_ref, gb_ref, gc_ref, hh_ref, dt_ref, convw_ref, convb_ref, dtb_ref,
                         alog_ref, dskip_ref, normg_ref, scw_ref, tril_ref, expand_ref,
                         y_ref, hout_ref, utail_ref, ctail_ref, state_ref, ext_ref, act_ref, extu_ref,
                         *, L, nc, G, R, P, N):
    RP = R * P
    D = G * RP
    GN = G * N
    CH = D + 2 * GN
    KC = convw_ref.shape[0]
    KS = scw_ref.shape[0]
    c = pl.program_id(1)

    @pl.when(c == 0)
    def _init():
        ext_ref[0:8, :] = jnp.zeros((8, CH), F32)
        extu_ref[0:8, :] = jnp.zeros((8, D), F32)
        state_ref[...] = jnp.zeros(state_ref.shape, F32)

    ext_ref[8:8 + L, 0:D] = xs_ref[...]
    ext_ref[8:8 + L, D:CH] = bc_ref[...]
    cw = min(512, CH)
    for j in range(CH // cw):
        cols = slice(j * cw, (j + 1) * cw)
        base = 8 - (KC - 1)
        acc = ext_ref[base:base + L, cols] * convw_ref[0:1, cols]
        for k in range(1, KC):
            acc = acc + ext_ref[base + k:base + k + L, cols] * convw_ref[k:k + 1, cols]
        act_ref[:, cols] = _silu(acc + convb_ref[:, cols])

    dt = _softplus(dt_ref[...] + dtb_ref[...])
    a = dt * (-jnp.exp(alog_ref[...]))
    acs = _sel_dot(tril_ref[...], a)
    dt_e = _dot_sel(dt, expand_ref[...])
    acs_e = _dot_sel(acs, expand_ref[...])
    acs_row = acs.T

    li = lax.broadcasted_iota(jnp.int32, (L, L), 0)
    si = lax.broadcasted_iota(jnp.int32, (L, L), 1)
    causal = si <= li
    ci = lax.broadcasted_iota(jnp.int32, (L, RP), 1)
    head_cols = [jnp.logical_and(ci >= r * P, ci < (r + 1) * P) for r in range(R)]

    for g in range(G):
        gc_ = slice(g * RP, (g + 1) * RP)
        xs_g = act_ref[:, gc_]
        b_g = act_ref[:, D + g * N:D + (g + 1) * N]
        c_g = act_ref[:, D + GN + g * N:D + GN + (g + 1) * N].astype(BF16)
        xdt = xs_g * dt_e[:, gc_]
        x_bd = jnp.concatenate([jnp.where(head_cols[r], xdt, 0.0) for r in range(R)], axis=0).astype(BF16)
        acs_g = acs_e[:, gc_]
        bt_g = b_g.T
        cb = jnp.dot(c_g, bt_g.astype(BF16), preferred_element_type=F32)

        ms, dbs = [], []
        for r in range(R):
            h = g * R + r
            col = acs[:, h:h + 1]
            row = acs_row[h:h + 1, :]
            ms.append(cb * jnp.exp(jnp.where(causal, col - row, -jnp.inf)))
            dbs.append(bt_g * jnp.exp(acs_row[h:h + 1, L - 1:L] - row))

        y = jnp.dot(jnp.concatenate(ms, axis=1).astype(BF16), x_bd, preferred_element_type=F32)
        h_prev = state_ref[g]
        y_off = jnp.dot(c_g, h_prev.astype(BF16), preferred_element_type=F32)
        dbt = jnp.concatenate(dbs, axis=1).astype(BF16)
        new_states = jnp.dot(dbt, x_bd, preferred_element_type=F32)
        state_ref[g] = h_prev * jnp.exp(acs_g[L - 1:L, :]) + new_states
        y = y + y_off * jnp.exp(acs_g)
        y = y + dskip_ref[:, gc_] * xs_g
        y = y * _silu(z_ref[:, gc_])
        y = y * lax.rsqrt(jnp.mean(y * y, -1, keepdims=True) + RMS_EPS)
        y_ref[:, gc_] = (y * normg_ref[:, gc_]).astype(BF16)

    cw = min(512, D)
    for j in range(D // cw):
        cols = slice(j * cw, (j + 1) * cw)
        extu_ref[8:8 + L, cols] = gc_ref[:, cols] * hh_ref[:, cols]
        base = 8 - (KS - 1)
        v = extu_ref[base:base + L, cols] * scw_ref[0:1, cols]
        for k in range(1, KS):
            v = v + extu_ref[base + k:base + k + L, cols] * scw_ref[k:k + 1, cols]
        y_ref[:, D + j * cw:D + (j + 1) * cw] = (gb_ref[:, cols] * v).astype(BF16)

    @pl.when(c == nc - 1)
    def _fin():
        utail_ref[0] = extu_ref[L:L + 8, :]
        ctail_ref[0] = ext_ref[L:L + 8, :]
        for g in range(G):
            hout_ref[0, g * R:(g + 1) * R] = state_ref[g].T.reshape(R, P, N)

    ext_ref[0:8, :] = ext_ref[L:L + 8, :]
    extu_ref[0:8, :] = extu_ref[L:L + 8, :]


def _mixer_consts(D, P):
    expand = (jnp.arange(V7X_LANES)[:, None] == (jnp.arange(D) // P)[None, :]).astype(BF16)
    return expand


def _proj_specs(rows, D, row_block):
    return [pl.BlockSpec((rows, D), functools.partial(lambda j, *idx: (row_block(*idx), j), j)) for j in (0, 1, 2)] * 2


def prompt_mixer(proj_a, proj_b, dt_raw, nb, T, cfg, wts):
    D, G, R, P, N = cfg["D"], cfg["G"], cfg["R"], cfg["P"], cfg["N"]
    H = G * R
    GN = G * N
    CH = D + 2 * GN
    L = SSD_CHUNK
    nc = T // L
    assert T % L == 0 and L == V7X_LANES and 2 * GN == D and R * P * G == D
    tril = jnp.tril(jnp.ones((L, L), BF16))
    expand = _mixer_consts(D, P)
    consts = list(wts) + [tril, expand]

    def row_block(b, c):
        return b * nc + c

    in_specs = (_proj_specs(L, D, row_block) + [pl.BlockSpec((L, V7X_LANES), lambda b, c: (b * nc + c, 0))]
                + [pl.BlockSpec(a.shape, functools.partial(lambda n, b, c: (0,) * n, a.ndim)) for a in consts])
    return pl.pallas_call(
        functools.partial(_prompt_mixer_kernel, L=L, nc=nc, G=G, R=R, P=P, N=N),
        grid=(nb, nc),
        in_specs=in_specs,
        out_specs=[pl.BlockSpec((L, 2 * D), lambda b, c: (b * nc + c, 0)),
                   pl.BlockSpec((1, H, P, N), lambda b, c: (b, 0, 0, 0)),
                   pl.BlockSpec((1, 8, D), lambda b, c: (b, 0, 0)),
                   pl.BlockSpec((1, 8, CH), lambda b, c: (b, 0, 0))],
        out_shape=[jax.ShapeDtypeStruct((nb * T, 2 * D), BF16),
                   jax.ShapeDtypeStruct((nb, H, P, N), F32),
                   jax.ShapeDtypeStruct((nb, 8, D), F32),
                   jax.ShapeDtypeStruct((nb, 8, CH), F32)],
        scratch_shapes=[pltpu.VMEM((G, N, R * P), F32), pltpu.VMEM((L + 8, CH), F32),
                        pltpu.VMEM((L, CH), F32), pltpu.VMEM((L + 8, D), F32)],
        compiler_params=_params("parallel", "arbitrary"),
    )(*([proj_a] * 3 + [proj_b] * 3 + [dt_raw] + consts))


def _sample_mixer_kernel(*refs, bb, T, G, R, P, N, n_prev):
    (z_ref, xs_ref, bc_ref, gb_ref, gc_ref, hh_ref, dt_ref, convw_ref, convb_ref, dtb_ref, alog_ref,
     dskip_ref, normg_ref, scw_ref, btril_ref, expand_ref, expand_t_ref, rep_ref,
     h0_ref, ctx_ref, scctx_ref) = refs[:21]
    prev_refs = refs[21:21 + n_prev]
    y_ref, hout_ref, utail_ref, ctail_ref, ext_ref, act_ref, extu_ref = refs[21 + n_prev:]
    H = G * R
    RP = R * P
    D = G * RP
    GN = G * N
    CH = D + 2 * GN
    HT = H * T
    KC = convw_ref.shape[0]
    KS = scw_ref.shape[0]
    rows = bb * T

    ext_ref[:, 8 - (KC - 1):8, :] = ctx_ref[...]
    ext_ref[:, 8:8 + T, 0:D] = xs_ref[...].reshape(bb, T, D)
    ext_ref[:, 8:8 + T, D:CH] = bc_ref[...].reshape(bb, T, 2 * GN)
    ctail_ref[...] = ext_ref[:, T:T + 8, :]
    cw = min(512, CH)
    for j in range(CH // cw):
        cols = slice(j * cw, (j + 1) * cw)
        base = 8 - (KC - 1)
        acc = ext_ref[:, base:base + T, cols] * convw_ref[0:1, cols]
        for k in range(1, KC):
            acc = acc + ext_ref[:, base + k:base + k + T, cols] * convw_ref[k:k + 1, cols]
        act_ref[:, cols] = _silu(acc + convb_ref[:, cols]).reshape(rows, cw)

    dt = _softplus(dt_ref[...] + dtb_ref[...])
    a = dt * (-jnp.exp(alog_ref[...]))
    acs = _sel_dot(btril_ref[...], a)
    dt_e = _dot_sel(dt, expand_ref[...])
    acs_e = _dot_sel(acs, expand_ref[...])
    acs_e3 = acs_e.reshape(bb, T, D)
    last_e = acs_e3[:, T - 1:T, :]
    xs_act = act_ref[:, 0:D]
    xdt = xs_act * dt_e
    xde = xdt * jnp.exp(last_e - acs_e3).reshape(rows, D)
    a1 = _dot_sel(acs, expand_t_ref[...]).reshape(bb, T, HT)
    l3 = lax.broadcasted_iota(jnp.int32, (bb, T, HT), 1)
    s3 = lax.broadcasted_iota(jnp.int32, (bb, T, HT), 2) & (T - 1)
    a2 = jnp.sum(jnp.where(l3 == s3, a1, 0.0), axis=1, keepdims=True)
    decay = jnp.exp(jnp.where(s3 <= l3, a1 - a2, -jnp.inf)).reshape(rows, HT)
    cd_heads = jnp.exp(acs)

    ri = lax.broadcasted_iota(jnp.int32, (HT, D), 0)
    ci = lax.broadcasted_iota(jnp.int32, (HT, D), 1)
    bd_mask = (ri >> (T.bit_length() - 1)) == (ci >> (P.bit_length() - 1))
    lane_g = lax.broadcasted_iota(jnp.int32, (T, HT), 1) >> ((R * T).bit_length() - 1)

    y_diag, y_off = [], []
    for b in range(bb):
        rs = slice(b * T, (b + 1) * T)
        b_pieces = [act_ref[rs, D + g * N:D + (g + 1) * N] for g in range(G)]
        c_st = jnp.concatenate([act_ref[rs, D + GN + g * N:D + GN + (g + 1) * N] for g in range(G)],
                               axis=0).astype(BF16)
        b_st = jnp.concatenate(b_pieces, axis=0).astype(BF16)
        call = lax.dot_general(c_st, b_st, NT_DIMS, preferred_element_type=F32)
        q = _dot_sel(call, rep_ref[...])
        cb = jnp.zeros((T, HT), F32)
        for g in range(G):
            cb = cb + jnp.where(lane_g == g, q[g * T:(g + 1) * T], 0.0)
        mcat = (cb * decay[rs]).astype(BF16)
        x_bd = jnp.where(bd_mask, jnp.concatenate([xdt[rs]] * H, axis=0), 0.0).astype(BF16)
        y_diag.append(jnp.dot(mcat, x_bd, preferred_element_type=F32))

        h0_b = h0_ref[b].reshape(D, N)
        yy = lax.dot_general(c_st, h0_b.astype(BF16), NT_DIMS, preferred_element_type=F32)
        y_off.append(jnp.concatenate([yy[g * T:(g + 1) * T, g * RP:(g + 1) * RP] for g in range(G)], axis=1))

        x_bd2 = jnp.where(bd_mask, jnp.concatenate([xde[rs]] * H, axis=0), 0.0).astype(BF16)
        b_rep = jnp.concatenate([b_pieces[h // R] for h in range(H)], axis=0).astype(BF16)
        new_states = lax.dot_general(x_bd2, b_rep, TN_DIMS, preferred_element_type=F32)
        r_last = b * T + T - 1
        cd = jnp.concatenate([jnp.broadcast_to(cd_heads[r_last:r_last + 1, h:h + 1], (P, N)) for h in range(H)],
                             axis=0)
        hout_ref[n_prev, b] = (h0_b * cd + new_states).reshape(H, P, N)
    for s, prev_ref in enumerate(prev_refs):
        hout_ref[s] = prev_ref[...]

    y = jnp.concatenate(y_diag, axis=0) + jnp.concatenate(y_off, axis=0) * jnp.exp(acs_e)
    y = y + dskip_ref[...] * xs_act
    y = y * _silu(z_ref[...])
    for g in range(G):
        gc_ = slice(g * RP, (g + 1) * RP)
        yg = y[:, gc_]
        yg = yg * lax.rsqrt(jnp.mean(yg * yg, -1, keepdims=True) + RMS_EPS)
        y_ref[:, gc_] = (yg * normg_ref[:, gc_]).astype(BF16)

    extu_ref[:, 8 - (KS - 1):8, :] = scctx_ref[...]
    cw = min(512, D)
    for j in range(D // cw):
        cols = slice(j * cw, (j + 1) * cw)
        extu_ref[:, 8:8 + T, cols] = (gc_ref[:, cols] * hh_ref[:, cols]).reshape(bb, T, cw)
        base = 8 - (KS - 1)
        v = extu_ref[:, base:base + T, cols] * scw_ref[0:1, cols]
        for k in range(1, KS):
            v = v + extu_ref[:, base + k:base + k + T, cols] * scw_ref[k:k + 1, cols]
        y_ref[:, D + j * cw:D + (j + 1) * cw] = (gb_ref[:, cols] * v.reshape(rows, cw)).astype(BF16)
    utail_ref[...] = extu_ref[:, T:T + 8, :]


def sample_mixer(proj_a, proj_b, dt_raw, row0, nb, T, cfg, wts, state_ssd, state_ssd_conv, state_short_conv,
                 layer_e, prev_states=(), bb=4):
    D, G, R, P, N = cfg["D"], cfg["G"], cfg["R"], cfg["P"], cfg["N"]
    H = G * R
    GN = G * N
    CH = D + 2 * GN
    bb = _tile(nb, bb, 2)
    rows = bb * T
    pow2 = lambda v: v & (v - 1) == 0
    assert T == V7X_SUBLANES and row0 % rows == 0 and 2 * GN == D and R * P * G == D
    assert pow2(P) and pow2(R)
    rb0 = row0 // rows
    kc1, ks1 = state_ssd_conv.shape[2], state_short_conv.shape[2]
    assert kc1 <= T and ks1 <= T
    seq = jnp.arange(rows) // T
    btril = jnp.logical_and(seq[:, None] == seq[None, :], jnp.arange(rows)[:, None] >= jnp.arange(rows)[None, :])
    expand = _mixer_consts(D, P)
    lane_h = jnp.arange(H * T) // T
    expand_t = (jnp.arange(V7X_LANES)[:, None] == lane_h[None, :]).astype(BF16)
    rep = jnp.logical_and((jnp.arange(G * T) // T)[:, None] == (lane_h // R)[None, :],
                          (jnp.arange(G * T) % T)[:, None] == (jnp.arange(H * T) % T)[None, :]).astype(BF16)
    consts = list(wts) + [btril.astype(BF16), expand, expand_t, rep]
    n_prev = len(prev_states)

    in_specs = (_proj_specs(rows, D, lambda i: rb0 + i) + [pl.BlockSpec((rows, V7X_LANES), lambda i: (rb0 + i, 0))]
                + [pl.BlockSpec(a.shape, functools.partial(lambda n, i: (0,) * n, a.ndim)) for a in consts]
                + [pl.BlockSpec((None, bb, H, P, N), lambda i: (layer_e, i, 0, 0, 0)),
                   pl.BlockSpec((None, bb, kc1, CH), lambda i: (layer_e, i, 0, 0)),
                   pl.BlockSpec((None, bb, ks1, D), lambda i: (layer_e, i, 0, 0))]
                + [pl.BlockSpec((bb, H, P, N), lambda i: (i, 0, 0, 0))] * n_prev)
    return pl.pallas_call(
        functools.partial(_sample_mixer_kernel, bb=bb, T=T, G=G, R=R, P=P, N=N, n_prev=n_prev),
        grid=(nb // bb,),
        in_specs=in_specs,
        out_specs=[pl.BlockSpec((rows, 2 * D), lambda i: (i, 0)),
                   pl.BlockSpec((n_prev + 1, bb, H, P, N), lambda i: (0, i, 0, 0, 0)),
                   pl.BlockSpec((bb, 8, D), lambda i: (i, 0, 0)),
                   pl.BlockSpec((bb, 8, CH), lambda i: (i, 0, 0))],
        out_shape=[jax.ShapeDtypeStruct((nb * T, 2 * D), BF16),
                   jax.ShapeDtypeStruct((n_prev + 1, nb, H, P, N), F32),
                   jax.ShapeDtypeStruct((nb, 8, D), F32),
                   jax.ShapeDtypeStruct((nb, 8, CH), F32)],
        scratch_shapes=[pltpu.VMEM((bb, 8 + T, CH), F32), pltpu.VMEM((rows, CH), F32),
                        pltpu.VMEM((bb, 8 + T, D), F32)],
        compiler_params=_params("parallel"),
    )(*([proj_a] * 3 + [proj_b] * 3 + [dt_raw] + consts + [state_ssd, state_ssd_conv, state_short_conv]
        + list(prev_states)))


def _pool_sample_kernel(st_ref, x_ref, p_ref, ext_ref, *, start_pos, dg):
    T = x_ref.shape[1]
    ext_ref[:, 16 - POOL_CTX:16, :] = st_ref[...]
    ext_ref[:, 16:16 + T, :] = x_ref[...]
    pos = start_pos + lax.broadcasted_iota(jnp.int32, (1, T, 1), 1)
    for gi, w in enumerate(POOL_WINDOWS):
        cols = slice(gi * dg, (gi + 1) * dg)
        s = ext_ref[:, 16:16 + T, cols]
        for j in range(1, w):
            s = s + ext_ref[:, 16 - j:16 - j + T, cols]
        cnt = jnp.minimum(pos + 1, w).astype(F32)
        p_ref[:, :, cols] = (s / cnt - x_ref[:, :, cols]).astype(BF16)


def pool_sample(state_pool, layer_o, x3, dg):
    B, T, D = x3.shape
    bb = _tile(B, 16, 1)
    return pl.pallas_call(
        functools.partial(_pool_sample_kernel, start_pos=PAST_LEN, dg=dg),
        grid=(B // bb,),
        in_specs=[pl.BlockSpec((None, bb, POOL_CTX, D), lambda i: (layer_o, i, 0, 0)),
                  pl.BlockSpec((bb, T, D), lambda i: (i, 0, 0))],
        out_specs=pl.BlockSpec((bb, T, D), lambda i: (i, 0, 0)),
        out_shape=jax.ShapeDtypeStruct((B, T, D), BF16),
        scratch_shapes=[pltpu.VMEM((bb, 16 + T, D), F32)],
        compiler_params=_params("parallel"),
    )(state_pool, x3)


def _pool_ln_kernel(x_ref, halo_ref, ps_ref, w_ref, scale_ref, g_ref, b_ref, of_ref, ob_ref, ext_ref, p_ref,
                    *, n_prompt_tiles, tiles_per_seq, dg, alpha):
    i = pl.program_id(0)
    bm = x_ref.shape[0]

    @pl.when(i < n_prompt_tiles)
    def _prompt():
        first = (i % tiles_per_seq) == 0
        ext_ref[0:16, :] = jnp.where(first, 0.0, halo_ref[...])
        ext_ref[16:16 + bm, :] = x_ref[...]
        pos = (i % tiles_per_seq) * bm + lax.broadcasted_iota(jnp.int32, (bm, 1), 0)
        for gi, w in enumerate(POOL_WINDOWS):
            cols = slice(gi * dg, (gi + 1) * dg)
            s = ext_ref[16:16 + bm, cols]
            for j in range(1, w):
                s = s + ext_ref[16 - j:16 - j + bm, cols]
            cnt = jnp.minimum(pos + 1, w).astype(F32)
            p_ref[:, cols] = (s / cnt - x_ref[:, cols]).astype(BF16)

    @pl.when(i >= n_prompt_tiles)
    def _sample():
        p_ref[...] = ps_ref[...]

    for gi in range(len(POOL_WINDOWS)):
        cols = slice(gi * dg, (gi + 1) * dg)
        m = jnp.dot(p_ref[:, cols], w_ref[gi], preferred_element_type=F32) * scale_ref[:, cols]
        ext_ref[16:16 + bm, cols] = alpha * x_ref[:, cols] + m
    out = _layer_norm(ext_ref[16:16 + bm, :], g_ref[...], b_ref[...])
    of_ref[...] = out
    ob_ref[...] = out.astype(BF16)


def pool_ln(x, p_sample, w_pool, scale, g, b, n_prompt_rows, T, alpha, bm=512):
    M, D = x.shape
    dg = D // len(POOL_WINDOWS)
    bm = _tile(math.gcd(math.gcd(n_prompt_rows, M - n_prompt_rows), T), bm, 16)
    npt = n_prompt_rows // bm
    hb = bm // 16
    return pl.pallas_call(
        functools.partial(_pool_ln_kernel, n_prompt_tiles=npt, tiles_per_seq=T // bm, dg=dg, alpha=alpha),
        grid=(M // bm,),
        in_specs=[pl.BlockSpec((bm, D), lambda i: (i, 0)),
                  pl.BlockSpec((16, D), lambda i: (jnp.maximum(i * hb - 1, 0), 0)),
                  pl.BlockSpec((bm, D), lambda i: (jnp.maximum(i - npt, 0), 0)),
                  pl.BlockSpec(w_pool.shape, lambda i: (0, 0, 0)),
                  pl.BlockSpec((1, D), lambda i: (0, 0)),
                  pl.BlockSpec((1, D), lambda i: (0, 0)),
                  pl.BlockSpec((1, D), lambda i: (0, 0))],
        out_specs=[pl.BlockSpec((bm, D), lambda i: (i, 0)), pl.BlockSpec((bm, D), lambda i: (i, 0))],
        out_shape=[jax.ShapeDtypeStruct((M, D), F32), jax.ShapeDtypeStruct((M, D), BF16)],
        scratch_shapes=[pltpu.VMEM((bm + 16, D), F32), pltpu.VMEM((bm, D), BF16)],
        compiler_params=_params("arbitrary"),
    )(x, x, p_sample, w_pool, scale, g, b)


def _softmax_rows(s):
    m = jnp.max(s, -1, keepdims=True)
    e = jnp.exp(s - m)
    return e / jnp.sum(e, -1, keepdims=True)


def _attn_prompt_kernel(q_ref, k_ref, v_ref, o_ref, *, scale):
    s = lax.dot_general(q_ref[...], k_ref[...].astype(BF16), (((1,), (1,)), ((), ())),
                        preferred_element_type=F32) * scale
    p = _softmax_rows(s).astype(BF16)
    o_ref[...] = jnp.dot(p, v_ref[...].astype(BF16), preferred_element_type=F32).astype(BF16)


def attn_prompt(q, mk, mv, layer, nb, T, mem_len, heads, tq=1024):
    M, D = q.shape
    hd = D // heads
    tq = _tile(T, tq, V7X_SUBLANES)
    nt = T // tq
    return pl.pallas_call(
        functools.partial(_attn_prompt_kernel, scale=hd ** -0.5),
        grid=(nb, heads, nt),
        in_specs=[pl.BlockSpec((tq, hd), lambda b, h, t: (b * nt + t, h)),
                  pl.BlockSpec((None, mem_len, hd), lambda b, h, t: (layer, b, h)),
                  pl.BlockSpec((None, mem_len, hd), lambda b, h, t: (layer, b, h))],
        out_specs=pl.BlockSpec((tq, hd), lambda b, h, t: (b * nt + t, h)),
        out_shape=jax.ShapeDtypeStruct((M, D), BF16),
        input_output_aliases={0: 0},
        compiler_params=_params("parallel", "parallel", "parallel"),
    )(q, mk, mv)


def _attn_sample_kernel(q_ref, k_hbm, v_hbm, o_ref, kbuf, vbuf, sem, *, layer, scale, T):
    _, heads, bb, mem_len, hd = kbuf.shape
    i = pl.program_id(0)
    n = pl.num_programs(0)

    def copies(step, slot):
        cps = []
        for h in range(heads):
            for t, (src, dst) in enumerate(((k_hbm, kbuf), (v_hbm, vbuf))):
                cps.append(pltpu.make_async_copy(src.at[layer, pl.ds(step * bb, bb), :, h, :],
                                                 dst.at[slot, h], sem.at[slot, t, h]))
        return cps

    @pl.when(i == 0)
    def _():
        for cp in copies(0, 0):
            cp.start()

    slot = i % 2

    @pl.when(i + 1 < n)
    def _():
        for cp in copies(i + 1, 1 - slot):
            cp.start()

    for cp in copies(i, slot):
        cp.wait()

    slots = bb * heads
    q_all = q_ref[...].astype(F32)
    st = jnp.zeros((mem_len, slots * T), F32)
    for b in range(bb):
        for h in range(heads):
            s_ = b * heads + h
            q_bh = q_all[b * T:(b + 1) * T, h * hd:(h + 1) * hd]
            parts = []
            if s_:
                parts.append(jnp.zeros((s_ * T, hd), F32))
            parts.append(q_bh)
            if s_ + 1 < slots:
                parts.append(jnp.zeros(((slots - s_ - 1) * T, hd), F32))
            q_slot = jnp.concatenate(parts, axis=0).astype(BF16)
            st = st + lax.dot_general(kbuf[slot, h, b].astype(BF16), q_slot, (((1,), (1,)), ((), ())),
                                      preferred_element_type=F32)
    st = st * scale
    e = jnp.exp(st - jnp.max(st, 0, keepdims=True))
    p_all = (e / jnp.sum(e, 0, keepdims=True)).T
    rows = []
    for b in range(bb):
        outs = []
        for h in range(heads):
            s_ = b * heads + h
            p = p_all[s_ * T:(s_ + 1) * T, :].astype(BF16)
            outs.append(jnp.dot(p, vbuf[slot, h, b].astype(BF16), preferred_element_type=F32))
        rows.append(jnp.concatenate(outs, axis=1))
    o_ref[...] = jnp.concatenate(rows, axis=0).astype(BF16)


def attn_sample(q, cache_k, cache_v, layer, row0, T):
    _, B, mem_len, heads, hd = cache_k.shape
    M, D = q.shape
    bb = V7X_LANES // (heads * T)
    assert bb * heads * T == V7X_LANES and B % bb == 0
    rows = bb * T
    assert row0 % rows == 0 and rows % 16 == 0
    rb0 = row0 // rows
    return pl.pallas_call(
        functools.partial(_attn_sample_kernel, layer=layer, scale=hd ** -0.5, T=T),
        grid=(B // bb,),
        in_specs=[pl.BlockSpec((rows, D), lambda i: (rb0 + i, 0)),
                  pl.BlockSpec(memory_space=pl.ANY),
                  pl.BlockSpec(memory_space=pl.ANY)],
        out_specs=pl.BlockSpec((rows, D), lambda i: (rb0 + i, 0)),
        out_shape=jax.ShapeDtypeStruct((M, D), BF16),
        input_output_aliases={0: 0},
        scratch_shapes=[pltpu.VMEM((2, heads, bb, mem_len, hd), F32),
                        pltpu.VMEM((2, heads, bb, mem_len, hd), F32),
                        pltpu.SemaphoreType.DMA((2, 2, heads))],
        compiler_params=_params("arbitrary"),
    )(q, cache_k, cache_v)


def kernel(x_prompt, x_sample, state_ssd, state_ssd_conv, state_short_conv, state_pool, cache_mem_k, cache_mem_v, mem_prompt, w_in_even, ssd_conv_w, ssd_conv_b, ssd_dt_bias, ssd_a_log, ssd_d, ssd_norm_g, sc_conv_w, w_out_even, w_pool, pool_scale, wq_x, wk_x, wv_x, wo_x, w_up, w_down, ln_g, ln_b):
    BP, T, D = x_prompt.shape
    BS, TS, _ = x_sample.shape
    depth = wq_x.shape[0]
    n_even = w_in_even.shape[0]
    _, _, H, P, N = state_ssd.shape
    CH = state_ssd_conv.shape[-1]
    G = (CH - D) // (2 * N)
    R = H // G
    cfg = dict(D=D, G=G, R=R, P=P, N=N)
    mem_len, heads, hd = cache_mem_k.shape[2:]
    dg = w_pool.shape[-1]
    alpha = (2.0 * depth) ** 0.25
    MP, MS = BP * T, BS * TS
    o_dt = D + CH
    assert ssd_d.shape[-1] == H and H <= V7X_LANES and heads * hd == D

    xf = jnp.concatenate([x_prompt.reshape(MP, D), x_sample.reshape(MS, D)], axis=0)
    xb = xf.astype(BF16)
    mem_b = mem_prompt.reshape(BP * mem_len, D).astype(BF16)
    mk2, mk5 = kv_proj(mem_b, wk_x, BP, heads)
    mv2, mv5 = kv_proj(mem_b, wv_x, BP, heads)

    w_in_t = jnp.swapaxes(w_in_even, 1, 2)
    w_dt = jnp.pad(w_in_even[:, :, o_dt:o_dt + H], ((0, 0), (0, 0), (0, V7X_LANES - H))).astype(BF16)
    w_pool_b = w_pool.astype(BF16)

    def pad_lanes(v):
        return jnp.pad(v.astype(F32), (0, V7X_LANES - v.shape[0]))[None, :]

    def last_rows(x2, nb, t, n):
        return jnp.stack([x2[(b + 1) * t - n:(b + 1) * t] for b in range(nb)])

    kc, ks = ssd_conv_w.shape[1], sc_conv_w.shape[1]
    hs_p, hs_s, cbs_p, cbs_s, sbs_p, sbs_s, pbs_p, pbs_s = [], [], [], [], [], [], [], []
    h_s = None
    for layer in range(depth):
        ln = lambda j: (ln_g[layer, j][None, :], ln_b[layer, j][None, :])
        if layer % 2 == 0:
            e = layer // 2
            proj_a, w_out_b = matmul(xb, w_in_t, e, F32, transposed=True, col0=0, n_cols=o_dt,
                                     side=(w_out_even, e))
            proj_b = matmul(xb, w_in_t, e, F32, transposed=True, col0=o_dt + H, n_cols=3 * D)
            dt_raw = matmul(xb, w_dt, e, F32)
            wts = (ssd_conv_w[e], ssd_conv_b[e][None, :], pad_lanes(ssd_dt_bias[e]), pad_lanes(ssd_a_log[e]),
                   jnp.repeat(ssd_d[e].astype(F32), P)[None, :], ssd_norm_g[e][None, :], sc_conv_w[e])
            y_p, h_p, ut_p, ct_p = prompt_mixer(proj_a, proj_b, dt_raw, BP, T, cfg, wts)
            y_s, h_s, ut_s, ct_s = sample_mixer(proj_a, proj_b, dt_raw, MP, BS, TS, cfg, wts,
                                                state_ssd, state_ssd_conv, state_short_conv, e,
                                                prev_states=hs_s if e == n_even - 1 else ())
            hs_s.append(h_s[0])
            hs_p.append(h_p)
            cbs_p.append(ct_p[:, 8 - (kc - 1):])
            cbs_s.append(ct_s[:, 8 - (kc - 1):])
            sbs_p.append(ut_p[:, 8 - (ks - 1):])
            sbs_s.append(ut_s[:, 8 - (ks - 1):])
            xf, xb = matmul_ln(y_p, w_out_b, None, xf, *ln(0), alpha, x2=y_s)
        else:
            o = layer // 2
            xs3 = xf[MP:].reshape(BS, TS, D)
            pbs_p.append(last_rows(xf, BP, T, POOL_CTX))
            pbs_s.append(jnp.concatenate([state_pool[o][:, TS:], xs3], axis=1))
            p_s = pool_sample(state_pool, o, xs3, dg).reshape(MS, D)
            xf, xb = pool_ln(xf, p_s, w_pool_b[o], pool_scale[o][None, :], *ln(0), MP, T, alpha)

        q, wo_b = matmul(xb, wq_x, layer, BF16, side=(wo_x, layer))
        att = attn_prompt(q, mk2, mv2, layer, BP, T, mem_len, heads)
        att = attn_sample(att, cache_mem_k, cache_mem_v, layer, MP, TS)
        xf, xb = matmul_ln(att, wo_b, None, xf, *ln(1), alpha)

        hid, w_down_b = matmul(xb, w_up, layer, BF16, act="relu2", side=(w_down, layer))
        xf, xb = matmul_ln(hid, w_down_b, None, xf, *ln(2), alpha)

    return (xf[:MP].reshape(BP, T, D), xf[MP:].reshape(BS, TS, D),
            jnp.stack(hs_p), h_s, jnp.stack(cbs_p), jnp.stack(cbs_s),
            jnp.stack(sbs_p), jnp.stack(sbs_s), jnp.stack(pbs_p), jnp.stack(pbs_s), mk5, mv5)
```

```python
import functools
import math

import jax
import jax.numpy as jnp
from jax import lax
from jax.experimental import pallas as pl
from jax.experimental.pallas import tpu as pltpu

F32 = jnp.float32
BF16 = jnp.bfloat16
NT_DIMS = (((1,), (1,)), ((), ()))
TN_DIMS = (((0,), (0,)), ((), ()))

PAST_LEN = 16384
SSD_CHUNK = 128
POOL_WINDOWS = (2, 4, 8, 16)
POOL_CTX = 15
LN_EPS = 1e-5
RMS_EPS = 1e-5

V7X_LANES = 128
V7X_SUBLANES = 8
V7X_VMEM_BYTES = 64 * 1024 * 1024
VMEM_LIMIT = V7X_VMEM_BYTES - 8 * 1024 * 1024


def _tile(dim, pref, quantum):
    best = None
    t = quantum
    while t <= min(dim, pref):
        if dim % t == 0:
            best = t
        t += quantum
    return best if best is not None else dim


def _params(*sem):
    return pltpu.CompilerParams(dimension_semantics=sem, vmem_limit_bytes=VMEM_LIMIT)


def _silu(x):
    return x * jax.nn.sigmoid(x)


def _softplus(x):
    return jnp.maximum(x, 0.0) + jnp.log1p(jnp.exp(-jnp.abs(x)))


def _layer_norm(y, g, b):
    mu = jnp.mean(y, -1, keepdims=True)
    d = y - mu
    var = jnp.mean(d * d, -1, keepdims=True)
    return d * lax.rsqrt(var + LN_EPS) * g + b


def _split3(x):
    hi = x.astype(BF16)
    r = x - hi.astype(F32)
    mid = r.astype(BF16)
    lo = (r - mid.astype(F32)).astype(BF16)
    return hi, mid, lo


def _dot_sel(x, sel):
    hi, mid, lo = _split3(x)
    return (jnp.dot(hi, sel, preferred_element_type=F32) + jnp.dot(mid, sel, preferred_element_type=F32)
            + jnp.dot(lo, sel, preferred_element_type=F32))


def _sel_dot(sel, x):
    hi, mid, lo = _split3(x)
    return (jnp.dot(sel, hi, preferred_element_type=F32) + jnp.dot(sel, mid, preferred_element_type=F32)
            + jnp.dot(sel, lo, preferred_element_type=F32))


def _mm_kernel(*refs, act, transposed, has_side):
    if has_side:
        x_ref, w_ref, side_ref, o_ref, side_out_ref = refs[:5]
        scratch = refs[5:]
        side_out_ref[...] = side_ref[...].astype(BF16)
    else:
        x_ref, w_ref, o_ref = refs[:3]
        scratch = refs[3:]
    if scratch:
        wb_ref, = scratch

        @pl.when(pl.program_id(1) == 0)
        def _():
            wb_ref[...] = (w_ref[0] if transposed else w_ref[...]).astype(BF16)

        w = wb_ref[...]
    else:
        w = w_ref[...]
    acc = lax.dot_general(x_ref[...], w, NT_DIMS if transposed else (((1,), (0,)), ((), ())),
                          preferred_element_type=F32)
    if act == "relu2":
        acc = jnp.square(jnp.maximum(acc, 0.0))
    o_ref[...] = acc.astype(o_ref.dtype)


def matmul(x, w, layer, out_dtype, act=None, transposed=False, col0=0, n_cols=None, side=None, bm=1024, bn=1024):
    M, K = x.shape
    if transposed:
        assert w.dtype != BF16 and col0 % V7X_SUBLANES == 0
        N = n_cols
    else:
        assert col0 == 0 and n_cols is None
        N = w.shape[2]
    bm = _tile(M, bm, V7X_SUBLANES)
    bn = _tile(N, bn, V7X_LANES)
    n_i = M // bm
    if transposed:
        w_spec = pl.BlockSpec((pl.Element(1), pl.Element(bn), pl.Element(K)),
                              lambda j, i: (layer, pl.multiple_of(col0 + j * bn, V7X_SUBLANES), 0))
    else:
        w_spec = pl.BlockSpec((None, K, bn), lambda j, i: (layer, 0, j))
    in_specs = [pl.BlockSpec((bm, K), lambda j, i: (i, 0)), w_spec]
    out_specs = [pl.BlockSpec((bm, bn), lambda j, i: (i, j))]
    out_shape = [jax.ShapeDtypeStruct((M, N), out_dtype)]
    args = [x, w]
    if side is not None:
        s_arr, s_layer = side
        _, s_rows, s_cols = s_arr.shape
        n_steps = (N // bn) * n_i
        rb = next(r for r in range(16, s_rows + 1, 16) if s_rows % r == 0 and s_rows // r <= n_steps)
        last = s_rows // rb - 1
        in_specs.append(pl.BlockSpec((None, rb, s_cols), lambda j, i: (s_layer, jnp.minimum(j * n_i + i, last), 0)))
        out_specs.append(pl.BlockSpec((rb, s_cols), lambda j, i: (jnp.minimum(j * n_i + i, last), 0)))
        out_shape.append(jax.ShapeDtypeStruct((s_rows, s_cols), BF16))
        args.append(s_arr)
    scratch = [pltpu.VMEM((bn, K) if transposed else (K, bn), BF16)] if w.dtype != BF16 else []
    outs = pl.pallas_call(
        functools.partial(_mm_kernel, act=act, transposed=transposed, has_side=side is not None),
        grid=(N // bn, n_i),
        in_specs=in_specs,
        out_specs=out_specs,
        out_shape=out_shape,
        scratch_shapes=scratch,
        compiler_params=_params("arbitrary", "arbitrary"),
    )(*args)
    return outs[0] if side is None else outs


def _kv_proj_kernel(x_ref, w_ref, o2_ref, o5_ref, wb_ref, *, heads):
    @pl.when(pl.program_id(1) == 0)
    def _():
        wb_ref[...] = w_ref[...].astype(BF16)

    acc = jnp.dot(x_ref[...], wb_ref[...], preferred_element_type=F32)
    o2_ref[...] = acc
    hd = acc.shape[1] // heads
    for h in range(heads):
        o5_ref[0, :, h, :] = acc[:, h * hd:(h + 1) * hd]


def kv_proj(mem_b, w, nb, heads):
    M, K = mem_b.shape
    depth, _, N = w.shape
    mem_len = M // nb
    hd = N // heads
    return pl.pallas_call(
        functools.partial(_kv_proj_kernel, heads=heads),
        grid=(depth, nb),
        in_specs=[pl.BlockSpec((mem_len, K), lambda l, b: (b, 0)),
                  pl.BlockSpec((None, K, N), lambda l, b: (l, 0, 0), pipeline_mode=pl.Buffered(1))],
        out_specs=[pl.BlockSpec((None, mem_len, N), lambda l, b: (l, b, 0)),
                   pl.BlockSpec((None, 1, mem_len, heads, hd), lambda l, b: (l, b, 0, 0, 0))],
        out_shape=[jax.ShapeDtypeStruct((depth, M, N), F32),
                   jax.ShapeDtypeStruct((depth, nb, mem_len, heads, hd), F32)],
        scratch_shapes=[pltpu.VMEM((K, N), BF16)],
        compiler_params=_params("parallel", "arbitrary"),
    )(mem_b, w)


def _mm_ln_kernel(*refs, alpha, n_first, two_x, two_res):
    refs = list(refs)
    x_refs = [refs.pop(0) for _ in range(2 if two_x else 1)]
    w_ref = refs.pop(0)
    res_refs = [refs.pop(0) for _ in range(2 if two_res else 1)]
    g_ref, b_ref, of_ref, ob_ref = refs

    bm = of_ref.shape[0]
    n_chunks = 2 if bm >= 512 else 1
    ch = bm // n_chunks

    def body(src_ref, res_ref):
        for c in range(n_chunks):
            rows = slice(c * ch, (c + 1) * ch)
            m = jnp.dot(src_ref[rows, :], w_ref[...], preferred_element_type=F32)
            out = _layer_norm(alpha * res_ref[rows, :] + m, g_ref[...], b_ref[...])
            of_ref[rows, :] = out
            ob_ref[rows, :] = out.astype(BF16)

    if not (two_x or two_res):
        body(x_refs[0], res_refs[0])
    else:
        i = pl.program_id(0)

        @pl.when(i < n_first)
        def _():
            body(x_refs[0], res_refs[0])

        @pl.when(i >= n_first)
        def _():
            body(x_refs[-1], res_refs[-1])


def matmul_ln(x, w, layer, res, g, b, alpha, x2=None, res2=None):
    M1, K = x.shape
    M2 = 0 if x2 is None else x2.shape[0]
    M = M1 + M2
    N = w.shape[-1]
    assert res2 is None or (x2 is not None and res.shape[0] == M1 and res2.shape[0] == M2)
    n_x = 1 if x2 is None else 2
    n_res = 1 if res2 is None else 2
    w_bytes = K * N * 2
    per_row = n_x * 2 * (K * 2) + n_res * 2 * (N * 4) + 2 * (N * 4) + 2 * (N * 2) + 2 * (N * 4)
    bm = _tile(math.gcd(M1, M2) if M2 else M1, min(512, (VMEM_LIMIT - w_bytes) // per_row), 2 * V7X_SUBLANES)
    n_first = M1 // bm

    def row_specs(width, two):
        if not two:
            return [pl.BlockSpec((bm, width), lambda i: (i, 0))]
        return [pl.BlockSpec((bm, width), lambda i: (jnp.minimum(i, n_first - 1), 0)),
                pl.BlockSpec((bm, width), lambda i: (jnp.maximum(i - n_first, 0), 0))]

    if layer is None:
        w_spec = pl.BlockSpec((K, N), lambda i: (0, 0), pipeline_mode=pl.Buffered(1))
    else:
        w_spec = pl.BlockSpec((None, K, N), lambda i: (layer, 0, 0), pipeline_mode=pl.Buffered(1))
    return pl.pallas_call(
        functools.partial(_mm_ln_kernel, alpha=alpha, n_first=n_first, two_x=x2 is not None,
                          two_res=res2 is not None),
        grid=(M // bm,),
        in_specs=row_specs(K, x2 is not None) + [w_spec] + row_specs(N, res2 is not None) + [
            pl.BlockSpec((1, N), lambda i: (0, 0)),
            pl.BlockSpec((1, N), lambda i: (0, 0))],
        out_specs=[pl.BlockSpec((bm, N), lambda i: (i, 0)),
                   pl.BlockSpec((bm, N), lambda i: (i, 0))],
        out_shape=[jax.ShapeDtypeStruct((M, N), F32), jax.ShapeDtypeStruct((M, N), BF16)],
        compiler_params=_params("arbitrary"),
    )(*([x] if x2 is None else [x, x2]), w, *([res] if res2 is None else [res, res2]), g, b)


def _prompt_mixer_kernel(z_ref, xs_ref, bc_ref, gb_ref, gc_ref, hh_ref, dt_ref, convw_ref, convb_ref, dtb_ref,
                         alog_ref, dskip_ref, normg_ref, scw_ref, tril_ref, expand_ref,
                         y_ref, hout_ref, utail_ref, ctail_ref, state_ref, ext_ref, act_ref, extu_ref,
                         *, L, nc, G, R, P, N):
    RP = R * P
    D = G * RP
    GN = G * N
    CH = D + 2 * GN
    KC = convw_ref.shape[0]
    KS = scw_ref.shape[0]
    c = pl.program_id(1)

    @pl.when(c == 0)
    def _init():
        ext_ref[0:8, :] = jnp.zeros((8, CH), F32)
        extu_ref[0:8, :] = jnp.zeros((8, D), F32)
        state_ref[...] = jnp.zeros(state_ref.shape, F32)

    ext_ref[8:8 + L, 0:D] = xs_ref[...]
    ext_ref[8:8 + L, D:CH] = bc_ref[...]
    cw = min(512, CH)
    for j in range(CH // cw):
        cols = slice(j * cw, (j + 1) * cw)
        base = 8 - (KC - 1)
        acc = ext_ref[base:base + L, cols] * convw_ref[0:1, cols]
        for k in range(1, KC):
            acc = acc + ext_ref[base + k:base + k + L, cols] * convw_ref[k:k + 1, cols]
        act_ref[:, cols] = _silu(acc + convb_ref[:, cols])

    dt = _softplus(dt_ref[...] + dtb_ref[...])
    a = dt * (-jnp.exp(alog_ref[...]))
    acs = _sel_dot(tril_ref[...], a)
    dt_e = _dot_sel(dt, expand_ref[...])
    acs_e = _dot_sel(acs, expand_ref[...])
    acs_row = acs.T

    li = lax.broadcasted_iota(jnp.int32, (L, L), 0)
    si = lax.broadcasted_iota(jnp.int32, (L, L), 1)
    causal = si <= li
    ci = lax.broadcasted_iota(jnp.int32, (L, RP), 1)
    head_cols = [jnp.logical_and(ci >= r * P, ci < (r + 1) * P) for r in range(R)]

    for g in range(G):
        gc_ = slice(g * RP, (g + 1) * RP)
        xs_g = act_ref[:, gc_]
        b_g = act_ref[:, D + g * N:D + (g + 1) * N]
        c_g = act_ref[:, D + GN + g * N:D + GN + (g + 1) * N].astype(BF16)
        xdt = xs_g * dt_e[:, gc_]
        x_bd = jnp.concatenate([jnp.where(head_cols[r], xdt, 0.0) for r in range(R)], axis=0).astype(BF16)
        acs_g = acs_e[:, gc_]
        bt_g = b_g.T
        cb = jnp.dot(c_g, bt_g.astype(BF16), preferred_element_type=F32)

        ms, dbs = [], []
        for r in range(R):
            h = g * R + r
            col = acs[:, h:h + 1]
            row = acs_row[h:h + 1, :]
            ms.append(cb * jnp.exp(jnp.where(causal, col - row, -jnp.inf)))
            dbs.append(bt_g * jnp.exp(acs_row[h:h + 1, L - 1:L] - row))

        y = jnp.dot(jnp.concatenate(ms, axis=1).astype(BF16), x_bd, preferred_element_type=F32)
        h_prev = state_ref[g]
        y_off = jnp.dot(c_g, h_prev.astype(BF16), preferred_element_type=F32)
        dbt = jnp.concatenate(dbs, axis=1).astype(BF16)
        new_states = jnp.dot(dbt, x_bd, preferred_element_type=F32)
        state_ref[g] = h_prev * jnp.exp(acs_g[L - 1:L, :]) + new_states
        y = y + y_off * jnp.exp(acs_g)
        y = y + dskip_ref[:, gc_] * xs_g
        y = y * _silu(z_ref[:, gc_])
        y = y * lax.rsqrt(jnp.mean(y * y, -1, keepdims=True) + RMS_EPS)
        y_ref[:, gc_] = (y * normg_ref[:, gc_]).astype(BF16)

    cw = min(512, D)
    for j in range(D // cw):
        cols = slice(j * cw, (j + 1) * cw)
        extu_ref[8:8 + L, cols] = gc_ref[:, cols] * hh_ref[:, cols]
        base = 8 - (KS - 1)
        v = extu_ref[base:base + L, cols] * scw_ref[0:1, cols]
        for k in range(1, KS):
            v = v + extu_ref[base + k:base + k + L, cols] * scw_ref[k:k + 1, cols]
        y_ref[:, D + j * cw:D + (j + 1) * cw] = (gb_ref[:, cols] * v).astype(BF16)

    @pl.when(c == nc - 1)
    def _fin():
        utail_ref[0] = extu_ref[L:L + 8, :]
        ctail_ref[0] = ext_ref[L:L + 8, :]
        for g in range(G):
            hout_ref[0, g * R:(g + 1) * R] = state_ref[g].T.reshape(R, P, N)

    ext_ref[0:8, :] = ext_ref[L:L + 8, :]
    extu_ref[0:8, :] = extu_ref[L:L + 8, :]


def _mixer_consts(D, P):
    expand = (jnp.arange(V7X_LANES)[:, None] == (jnp.arange(D) // P)[None, :]).astype(BF16)
    return expand


def _proj_specs(rows, D, row_block):
    return [pl.BlockSpec((rows, D), functools.partial(lambda j, *idx: (row_block(*idx), j), j)) for j in (0, 1, 2)] * 2


def prompt_mixer(proj_a, proj_b, dt_raw, nb, T, cfg, wts):
    D, G, R, P, N = cfg["D"], cfg["G"], cfg["R"], cfg["P"], cfg["N"]
    H = G * R
    GN = G * N
    CH = D + 2 * GN
    L = SSD_CHUNK
    nc = T // L
    assert T % L == 0 and L == V7X_LANES and 2 * GN == D and R * P * G == D
    tril = jnp.tril(jnp.ones((L, L), BF16))
    expand = _mixer_consts(D, P)
    consts = list(wts) + [tril, expand]

    def row_block(b, c):
        return b * nc + c

    in_specs = (_proj_specs(L, D, row_block) + [pl.BlockSpec((L, V7X_LANES), lambda b, c: (b * nc + c, 0))]
                + [pl.BlockSpec(a.shape, functools.partial(lambda n, b, c: (0,) * n, a.ndim)) for a in consts])
    return pl.pallas_call(
        functools.partial(_prompt_mixer_kernel, L=L, nc=nc, G=G, R=R, P=P, N=N),
        grid=(nb, nc),
        in_specs=in_specs,
        out_specs=[pl.BlockSpec((L, 2 * D), lambda b, c: (b * nc + c, 0)),
                   pl.BlockSpec((1, H, P, N), lambda b, c: (b, 0, 0, 0)),
                   pl.BlockSpec((1, 8, D), lambda b, c: (b, 0, 0)),
                   pl.BlockSpec((1, 8, CH), lambda b, c: (b, 0, 0))],
        out_shape=[jax.ShapeDtypeStruct((nb * T, 2 * D), BF16),
                   jax.ShapeDtypeStruct((nb, H, P, N), F32),
                   jax.ShapeDtypeStruct((nb, 8, D), F32),
                   jax.ShapeDtypeStruct((nb, 8, CH), F32)],
        scratch_shapes=[pltpu.VMEM((G, N, R * P), F32), pltpu.VMEM((L + 8, CH), F32),
                        pltpu.VMEM((L, CH), F32), pltpu.VMEM((L + 8, D), F32)],
        compiler_params=_params("parallel", "arbitrary"),
    )(*([proj_a] * 3 + [proj_b] * 3 + [dt_raw] + consts))


def _sample_mixer_kernel(*refs, bb, T, G, R, P, N, n_prev):
    (z_ref, xs_ref, bc_ref, gb_ref, gc_ref, hh_ref, dt_ref, convw_ref, convb_ref, dtb_ref, alog_ref,
     dskip_ref, normg_ref, scw_ref, btril_ref, expand_ref, expand_t_ref, rep_ref,
     h0_ref, ctx_ref, scctx_ref) = refs[:21]
    prev_refs = refs[21:21 + n_prev]
    y_ref, hout_ref, utail_ref, ctail_ref, ext_ref, act_ref, extu_ref = refs[21 + n_prev:]
    H = G * R
    RP = R * P
    D = G * RP
    GN = G * N
    CH = D + 2 * GN
    HT = H * T
    KC = convw_ref.shape[0]
    KS = scw_ref.shape[0]
    rows = bb * T

    ext_ref[:, 8 - (KC - 1):8, :] = ctx_ref[...]
    ext_ref[:, 8:8 + T, 0:D] = xs_ref[...].reshape(bb, T, D)
    ext_ref[:, 8:8 + T, D:CH] = bc_ref[...].reshape(bb, T, 2 * GN)
    ctail_ref[...] = ext_ref[:, T:T + 8, :]
    cw = min(512, CH)
    for j in range(CH // cw):
        cols = slice(j * cw, (j + 1) * cw)
        base = 8 - (KC - 1)
        acc = ext_ref[:, base:base + T, cols] * convw_ref[0:1, cols]
        for k in range(1, KC):
            acc = acc + ext_ref[:, base + k:base + k + T, cols] * convw_ref[k:k + 1, cols]
        act_ref[:, cols] = _silu(acc + convb_ref[:, cols]).reshape(rows, cw)

    dt = _softplus(dt_ref[...] + dtb_ref[...])
    a = dt * (-jnp.exp(alog_ref[...]))
    acs = _sel_dot(btril_ref[...], a)
    dt_e = _dot_sel(dt, expand_ref[...])
    acs_e = _dot_sel(acs, expand_ref[...])
    acs_e3 = acs_e.reshape(bb, T, D)
    last_e = acs_e3[:, T - 1:T, :]
    xs_act = act_ref[:, 0:D]
    xdt = xs_act * dt_e
    xde = xdt * jnp.exp(last_e - acs_e3).reshape(rows, D)
    a1 = _dot_sel(acs, expand_t_ref[...]).reshape(bb, T, HT)
    l3 = lax.broadcasted_iota(jnp.int32, (bb, T, HT), 1)
    s3 = lax.broadcasted_iota(jnp.int32, (bb, T, HT), 2) & (T - 1)
    a2 = jnp.sum(jnp.where(l3 == s3, a1, 0.0), axis=1, keepdims=True)
    decay = jnp.exp(jnp.where(s3 <= l3, a1 - a2, -jnp.inf)).reshape(rows, HT)
    cd_heads = jnp.exp(acs)

    ri = lax.broadcasted_iota(jnp.int32, (HT, D), 0)
    ci = lax.broadcasted_iota(jnp.int32, (HT, D), 1)
    bd_mask = (ri >> (T.bit_length() - 1)) == (ci >> (P.bit_length() - 1))
    lane_g = lax.broadcasted_iota(jnp.int32, (T, HT), 1) >> ((R * T).bit_length() - 1)

    y_diag, y_off = [], []
    for b in range(bb):
        rs = slice(b * T, (b + 1) * T)
        b_pieces = [act_ref[rs, D + g * N:D + (g + 1) * N] for g in range(G)]
        c_st = jnp.concatenate([act_ref[rs, D + GN + g * N:D + GN + (g + 1) * N] for g in range(G)],
                               axis=0).astype(BF16)
        b_st = jnp.concatenate(b_pieces, axis=0).astype(BF16)
        call = lax.dot_general(c_st, b_st, NT_DIMS, preferred_element_type=F32)
        q = _dot_sel(call, rep_ref[...])
        cb = jnp.zeros((T, HT), F32)
        for g in range(G):
            cb = cb + jnp.where(lane_g == g, q[g * T:(g + 1) * T], 0.0)
        mcat = (cb * decay[rs]).astype(BF16)
        x_bd = jnp.where(bd_mask, jnp.concatenate([xdt[rs]] * H, axis=0), 0.0).astype(BF16)
        y_diag.append(jnp.dot(mcat, x_bd, preferred_element_type=F32))

        h0_b = h0_ref[b].reshape(D, N)
        yy = lax.dot_general(c_st, h0_b.astype(BF16), NT_DIMS, preferred_element_type=F32)
        y_off.append(jnp.concatenate([yy[g * T:(g + 1) * T, g * RP:(g + 1) * RP] for g in range(G)], axis=1))

        x_bd2 = jnp.where(bd_mask, jnp.concatenate([xde[rs]] * H, axis=0), 0.0).astype(BF16)
        b_rep = jnp.concatenate([b_pieces[h // R] for h in range(H)], axis=0).astype(BF16)
        new_states = lax.dot_general(x_bd2, b_rep, TN_DIMS, preferred_element_type=F32)
        r_last = b * T + T - 1
        cd = jnp.concatenate([jnp.broadcast_to(cd_heads[r_last:r_last + 1, h:h + 1], (P, N)) for h in range(H)],
                             axis=0)
        hout_ref[n_prev, b] = (h0_b * cd + new_states).reshape(H, P, N)
    for s, prev_ref in enumerate(prev_refs):
        hout_ref[s] = prev_ref[...]

    y = jnp.concatenate(y_diag, axis=0) + jnp.concatenate(y_off, axis=0) * jnp.exp(acs_e)
    y = y + dskip_ref[...] * xs_act
    y = y * _silu(z_ref[...])
    for g in range(G):
        gc_ = slice(g * RP, (g + 1) * RP)
        yg = y[:, gc_]
        yg = yg * lax.rsqrt(jnp.mean(yg * yg, -1, keepdims=True) + RMS_EPS)
        y_ref[:, gc_] = (yg * normg_ref[:, gc_]).astype(BF16)

    extu_ref[:, 8 - (KS - 1):8, :] = scctx_ref[...]
    cw = min(512, D)
    for j in range(D // cw):
        cols = slice(j * cw, (j + 1) * cw)
        extu_ref[:, 8:8 + T, cols] = (gc_ref[:, cols] * hh_ref[:, cols]).reshape(bb, T, cw)
        base = 8 - (KS - 1)
        v = extu_ref[:, base:base + T, cols] * scw_ref[0:1, cols]
        for k in range(1, KS):
            v = v + extu_ref[:, base + k:base + k + T, cols] * scw_ref[k:k + 1, cols]
        y_ref[:, D + j * cw:D + (j + 1) * cw] = (gb_ref[:, cols] * v.reshape(rows, cw)).astype(BF16)
    utail_ref[...] = extu_ref[:, T:T + 8, :]


def sample_mixer(proj_a, proj_b, dt_raw, row0, nb, T, cfg, wts, state_ssd, state_ssd_conv, state_short_conv,
                 layer_e, prev_states=(), bb=4):
    D, G, R, P, N = cfg["D"], cfg["G"], cfg["R"], cfg["P"], cfg["N"]
    H = G * R
    GN = G * N
    CH = D + 2 * GN
    bb = _tile(nb, bb, 2)
    rows = bb * T
    pow2 = lambda v: v & (v - 1) == 0
    assert T == V7X_SUBLANES and row0 % rows == 0 and 2 * GN == D and R * P * G == D
    assert pow2(P) and pow2(R)
    rb0 = row0 // rows
    kc1, ks1 = state_ssd_conv.shape[2], state_short_conv.shape[2]
    assert kc1 <= T and ks1 <= T
    seq = jnp.arange(rows) // T
    btril = jnp.logical_and(seq[:, None] == seq[None, :], jnp.arange(rows)[:, None] >= jnp.arange(rows)[None, :])
    expand = _mixer_consts(D, P)
    lane_h = jnp.arange(H * T) // T
    expand_t = (jnp.arange(V7X_LANES)[:, None] == lane_h[None, :]).astype(BF16)
    rep = jnp.logical_and((jnp.arange(G * T) // T)[:, None] == (lane_h // R)[None, :],
                          (jnp.arange(G * T) % T)[:, None] == (jnp.arange(H * T) % T)[None, :]).astype(BF16)
    consts = list(wts) + [btril.astype(BF16), expand, expand_t, rep]
    n_prev = len(prev_states)

    in_specs = (_proj_specs(rows, D, lambda i: rb0 + i) + [pl.BlockSpec((rows, V7X_LANES), lambda i: (rb0 + i, 0))]
                + [pl.BlockSpec(a.shape, functools.partial(lambda n, i: (0,) * n, a.ndim)) for a in consts]
                + [pl.BlockSpec((None, bb, H, P, N), lambda i: (layer_e, i, 0, 0, 0)),
                   pl.BlockSpec((None, bb, kc1, CH), lambda i: (layer_e, i, 0, 0)),
                   pl.BlockSpec((None, bb, ks1, D), lambda i: (layer_e, i, 0, 0))]
                + [pl.BlockSpec((bb, H, P, N), lambda i: (i, 0, 0, 0))] * n_prev)
    return pl.pallas_call(
        functools.partial(_sample_mixer_kernel, bb=bb, T=T, G=G, R=R, P=P, N=N, n_prev=n_prev),
        grid=(nb // bb,),
        in_specs=in_specs,
        out_specs=[pl.BlockSpec((rows, 2 * D), lambda i: (i, 0)),
                   pl.BlockSpec((n_prev + 1, bb, H, P, N), lambda i: (0, i, 0, 0, 0)),
                   pl.BlockSpec((bb, 8, D), lambda i: (i, 0, 0)),
                   pl.BlockSpec((bb, 8, CH), lambda i: (i, 0, 0))],
        out_shape=[jax.ShapeDtypeStruct((nb * T, 2 * D), BF16),
                   jax.ShapeDtypeStruct((n_prev + 1, nb, H, P, N), F32),
                   jax.ShapeDtypeStruct((nb, 8, D), F32),
                   jax.ShapeDtypeStruct((nb, 8, CH), F32)],
        scratch_shapes=[pltpu.VMEM((bb, 8 + T, CH), F32), pltpu.VMEM((rows, CH), F32),
                        pltpu.VMEM((bb, 8 + T, D), F32)],
        compiler_params=_params("parallel"),
    )(*([proj_a] * 3 + [proj_b] * 3 + [dt_raw] + consts + [state_ssd, state_ssd_conv, state_short_conv]
        + list(prev_states)))


def _pool_sample_kernel(st_ref, x_ref, p_ref, ext_ref, *, start_pos, dg):
    T = x_ref.shape[1]
    ext_ref[:, 16 - POOL_CTX:16, :] = st_ref[...]
    ext_ref[:, 16:16 + T, :] = x_ref[...]
    pos = start_pos + lax.broadcasted_iota(jnp.int32, (1, T, 1), 1)
    for gi, w in enumerate(POOL_WINDOWS):
        cols = slice(gi * dg, (gi + 1) * dg)
        s = ext_ref[:, 16:16 + T, cols]
        for j in range(1, w):
            s = s + ext_ref[:, 16 - j:16 - j + T, cols]
        cnt = jnp.minimum(pos + 1, w).astype(F32)
        p_ref[:, :, cols] = (s / cnt - x_ref[:, :, cols]).astype(BF16)


def pool_sample(state_pool, layer_o, x3, dg):
    B, T, D = x3.shape
    bb = _tile(B, 16, 1)
    return pl.pallas_call(
        functools.partial(_pool_sample_kernel, start_pos=PAST_LEN, dg=dg),
        grid=(B // bb,),
        in_specs=[pl.BlockSpec((None, bb, POOL_CTX, D), lambda i: (layer_o, i, 0, 0)),
                  pl.BlockSpec((bb, T, D), lambda i: (i, 0, 0))],
        out_specs=pl.BlockSpec((bb, T, D), lambda i: (i, 0, 0)),
        out_shape=jax.ShapeDtypeStruct((B, T, D), BF16),
        scratch_shapes=[pltpu.VMEM((bb, 16 + T, D), F32)],
        compiler_params=_params("parallel"),
    )(state_pool, x3)


POOL_TOP = 8 + 16


def _pool_ln_kernel(x_ref, halo_ref, ps_ref, w_ref, scale_ref, g_ref, b_ref, of_ref, ob_ref, ext_ref, tmp_ref, p_ref,
                    *, n_prompt_tiles, tiles_per_seq, dg, alpha):
    i = pl.program_id(0)
    bm = x_ref.shape[0]
    top = POOL_TOP
    end = top + bm

    @pl.when(i < n_prompt_tiles)
    def _prompt():
        first = (i % tiles_per_seq) == 0
        ext_ref[0:8, :] = jnp.zeros((8, ext_ref.shape[1]), F32)
        ext_ref[8:top, :] = jnp.where(first, 0.0, halo_ref[...])
        ext_ref[top:end, :] = x_ref[...]
        pos = (i % tiles_per_seq) * bm + lax.broadcasted_iota(jnp.int32, (bm, 1), 0)
        for gi, w in enumerate(POOL_WINDOWS):
            cols = slice(gi * dg, (gi + 1) * dg)
            src, src_cols, k, level = ext_ref, cols, 1, 0
            while 2 * k < w:
                dst = tmp_ref.at[level % 2]
                dst[0:8, :] = jnp.zeros((8, dg), F32)
                dst[8:end, :] = src[8:end, src_cols] + src[8 - k:end - k, src_cols]
                src, src_cols, k, level = dst, slice(None), 2 * k, level + 1
            s = src[top:end, src_cols] + src[top - k:end - k, src_cols]
            cnt = jnp.minimum(pos + 1, w).astype(F32)
            p_ref[:, cols] = (s / cnt - x_ref[:, cols]).astype(BF16)

    @pl.when(i >= n_prompt_tiles)
    def _sample():
        p_ref[...] = ps_ref[...]

    for gi in range(len(POOL_WINDOWS)):
        cols = slice(gi * dg, (gi + 1) * dg)
        m = jnp.dot(p_ref[:, cols], w_ref[gi], preferred_element_type=F32) * scale_ref[:, cols]
        ext_ref[top:end, cols] = alpha * x_ref[:, cols] + m
    out = _layer_norm(ext_ref[top:end, :], g_ref[...], b_ref[...])
    of_ref[...] = out
    ob_ref[...] = out.astype(BF16)


def pool_ln(x, p_sample, w_pool, scale, g, b, n_prompt_rows, T, alpha, bm=512):
    M, D = x.shape
    dg = D // len(POOL_WINDOWS)
    bm = _tile(math.gcd(math.gcd(n_prompt_rows, M - n_prompt_rows), T), bm, 16)
    npt = n_prompt_rows // bm
    hb = bm // 16
    return pl.pallas_call(
        functools.partial(_pool_ln_kernel, n_prompt_tiles=npt, tiles_per_seq=T // bm, dg=dg, alpha=alpha),
        grid=(M // bm,),
        in_specs=[pl.BlockSpec((bm, D), lambda i: (i, 0)),
                  pl.BlockSpec((16, D), lambda i: (jnp.maximum(i * hb - 1, 0), 0)),
                  pl.BlockSpec((bm, D), lambda i: (jnp.maximum(i - npt, 0), 0)),
                  pl.BlockSpec(w_pool.shape, lambda i: (0, 0, 0)),
                  pl.BlockSpec((1, D), lambda i: (0, 0)),
                  pl.BlockSpec((1, D), lambda i: (0, 0)),
                  pl.BlockSpec((1, D), lambda i: (0, 0))],
        out_specs=[pl.BlockSpec((bm, D), lambda i: (i, 0)), pl.BlockSpec((bm, D), lambda i: (i, 0))],
        out_shape=[jax.ShapeDtypeStruct((M, D), F32), jax.ShapeDtypeStruct((M, D), BF16)],
        scratch_shapes=[pltpu.VMEM((POOL_TOP + bm, D), F32), pltpu.VMEM((2, POOL_TOP + bm, dg), F32),
                        pltpu.VMEM((bm, D), BF16)],
        compiler_params=_params("arbitrary"),
    )(x, x, p_sample, w_pool, scale, g, b)


def _softmax_rows(s):
    m = jnp.max(s, -1, keepdims=True)
    e = jnp.exp(s - m)
    return e / jnp.sum(e, -1, keepdims=True)


def _attn_prompt_kernel(q_ref, k_ref, v_ref, o_ref, *, scale):
    s = lax.dot_general(q_ref[...], k_ref[...].astype(BF16), (((1,), (1,)), ((), ())),
                        preferred_element_type=F32) * scale
    p = _softmax_rows(s).astype(BF16)
    o_ref[...] = jnp.dot(p, v_ref[...].astype(BF16), preferred_element_type=F32).astype(BF16)


def attn_prompt(q, mk, mv, layer, nb, T, mem_len, heads, tq=1024):
    M, D = q.shape
    hd = D // heads
    tq = _tile(T, tq, V7X_SUBLANES)
    nt = T // tq
    return pl.pallas_call(
        functools.partial(_attn_prompt_kernel, scale=hd ** -0.5),
        grid=(nb, heads, nt),
        in_specs=[pl.BlockSpec((tq, hd), lambda b, h, t: (b * nt + t, h)),
                  pl.BlockSpec((None, mem_len, hd), lambda b, h, t: (layer, b, h)),
                  pl.BlockSpec((None, mem_len, hd), lambda b, h, t: (layer, b, h))],
        out_specs=pl.BlockSpec((tq, hd), lambda b, h, t: (b * nt + t, h)),
        out_shape=jax.ShapeDtypeStruct((M, D), BF16),
        input_output_aliases={0: 0},
        compiler_params=_params("parallel", "parallel", "parallel"),
    )(q, mk, mv)


def _attn_sample_kernel(q_ref, k_hbm, v_hbm, o_ref, kbuf, vbuf, sem, *, layer, scale, T):
    _, heads, bb, mem_len, hd = kbuf.shape
    i = pl.program_id(0)
    n = pl.num_programs(0)

    def copies(step, slot):
        cps = []
        for h in range(heads):
            for t, (src, dst) in enumerate(((k_hbm, kbuf), (v_hbm, vbuf))):
                cps.append(pltpu.make_async_copy(src.at[layer, pl.ds(step * bb, bb), :, h, :],
                                                 dst.at[slot, h], sem.at[slot, t, h]))
        return cps

    @pl.when(i == 0)
    def _():
        for cp in copies(0, 0):
            cp.start()

    slot = i % 2

    @pl.when(i + 1 < n)
    def _():
        for cp in copies(i + 1, 1 - slot):
            cp.start()

    for cp in copies(i, slot):
        cp.wait()

    slots = bb * heads
    q_all = q_ref[...].astype(F32)
    st = jnp.zeros((mem_len, slots * T), F32)
    for b in range(bb):
        for h in range(heads):
            s_ = b * heads + h
            q_bh = q_all[b * T:(b + 1) * T, h * hd:(h + 1) * hd]
            parts = []
            if s_:
                parts.append(jnp.zeros((s_ * T, hd), F32))
            parts.append(q_bh)
            if s_ + 1 < slots:
                parts.append(jnp.zeros(((slots - s_ - 1) * T, hd), F32))
            q_slot = jnp.concatenate(parts, axis=0).astype(BF16)
            st = st + lax.dot_general(kbuf[slot, h, b].astype(BF16), q_slot, (((1,), (1,)), ((), ())),
                                      preferred_element_type=F32)
    st = st * scale
    e = jnp.exp(st - jnp.max(st, 0, keepdims=True))
    p_all = (e / jnp.sum(e, 0, keepdims=True)).T
    rows = []
    for b in range(bb):
        outs = []
        for h in range(heads):
            s_ = b * heads + h
            p = p_all[s_ * T:(s_ + 1) * T, :].astype(BF16)
            outs.append(jnp.dot(p, vbuf[slot, h, b].astype(BF16), preferred_element_type=F32))
        rows.append(jnp.concatenate(outs, axis=1))
    o_ref[...] = jnp.concatenate(rows, axis=0).astype(BF16)


def attn_sample(q, cache_k, cache_v, layer, row0, T):
    _, B, mem_len, heads, hd = cache_k.shape
    M, D = q.shape
    bb = V7X_LANES // (heads * T)
    assert bb * heads * T == V7X_LANES and B % bb == 0
    rows = bb * T
    assert row0 % rows == 0 and rows % 16 == 0
    rb0 = row0 // rows
    return pl.pallas_call(
        functools.partial(_attn_sample_kernel, layer=layer, scale=hd ** -0.5, T=T),
        grid=(B // bb,),
        in_specs=[pl.BlockSpec((rows, D), lambda i: (rb0 + i, 0)),
                  pl.BlockSpec(memory_space=pl.ANY),
                  pl.BlockSpec(memory_space=pl.ANY)],
        out_specs=pl.BlockSpec((rows, D), lambda i: (rb0 + i, 0)),
        out_shape=jax.ShapeDtypeStruct((M, D), BF16),
        input_output_aliases={0: 0},
        scratch_shapes=[pltpu.VMEM((2, heads, bb, mem_len, hd), F32),
                        pltpu.VMEM((2, heads, bb, mem_len, hd), F32),
                        pltpu.SemaphoreType.DMA((2, 2, heads))],
        compiler_params=_params("arbitrary"),
    )(q, cache_k, cache_v)


def kernel(x_prompt, x_sample, state_ssd, state_ssd_conv, state_short_conv, state_pool, cache_mem_k, cache_mem_v, mem_prompt, w_in_even, ssd_conv_w, ssd_conv_b, ssd_dt_bias, ssd_a_log, ssd_d, ssd_norm_g, sc_conv_w, w_out_even, w_pool, pool_scale, wq_x, wk_x, wv_x, wo_x, w_up, w_down, ln_g, ln_b):
    BP, T, D = x_prompt.shape
    BS, TS, _ = x_sample.shape
    depth = wq_x.shape[0]
    n_even = w_in_even.shape[0]
    _, _, H, P, N = state_ssd.shape
    CH = state_ssd_conv.shape[-1]
    G = (CH - D) // (2 * N)
    R = H // G
    cfg = dict(D=D, G=G, R=R, P=P, N=N)
    mem_len, heads, hd = cache_mem_k.shape[2:]
    dg = w_pool.shape[-1]
    alpha = (2.0 * depth) ** 0.25
    MP, MS = BP * T, BS * TS
    o_dt = D + CH
    assert ssd_d.shape[-1] == H and H <= V7X_LANES and heads * hd == D

    xp2, xs2 = x_prompt.reshape(MP, D), x_sample.reshape(MS, D)
    xb = jnp.concatenate([xp2.astype(BF16), xs2.astype(BF16)], axis=0)
    xf = None
    mem_b = mem_prompt.reshape(BP * mem_len, D).astype(BF16)
    mk2, mk5 = kv_proj(mem_b, wk_x, BP, heads)
    mv2, mv5 = kv_proj(mem_b, wv_x, BP, heads)

    w_in_t = jnp.swapaxes(w_in_even, 1, 2)
    w_dt = jnp.pad(w_in_even[:, :, o_dt:o_dt + H], ((0, 0), (0, 0), (0, V7X_LANES - H))).astype(BF16)
    w_pool_b = w_pool.astype(BF16)

    def pad_lanes(v):
        return jnp.pad(v.astype(F32), (0, V7X_LANES - v.shape[0]))[None, :]

    def last_rows(x2, nb, t, n):
        return jnp.stack([x2[(b + 1) * t - n:(b + 1) * t] for b in range(nb)])

    kc, ks = ssd_conv_w.shape[1], sc_conv_w.shape[1]
    hs_p, hs_s, cbs_p, cbs_s, sbs_p, sbs_s, pbs_p, pbs_s = [], [], [], [], [], [], [], []
    h_s = None
    for layer in range(depth):
        ln = lambda j: (ln_g[layer, j][None, :], ln_b[layer, j][None, :])
        if layer % 2 == 0:
            e = layer // 2
            proj_a, w_out_b = matmul(xb, w_in_t, e, F32, transposed=True, col0=0, n_cols=o_dt,
                                     side=(w_out_even, e))
            proj_b = matmul(xb, w_in_t, e, F32, transposed=True, col0=o_dt + H, n_cols=3 * D)
            dt_raw = matmul(xb, w_dt, e, F32)
            wts = (ssd_conv_w[e], ssd_conv_b[e][None, :], pad_lanes(ssd_dt_bias[e]), pad_lanes(ssd_a_log[e]),
                   jnp.repeat(ssd_d[e].astype(F32), P)[None, :], ssd_norm_g[e][None, :], sc_conv_w[e])
            y_p, h_p, ut_p, ct_p = prompt_mixer(proj_a, proj_b, dt_raw, BP, T, cfg, wts)
            y_s, h_s, ut_s, ct_s = sample_mixer(proj_a, proj_b, dt_raw, MP, BS, TS, cfg, wts,
                                                state_ssd, state_ssd_conv, state_short_conv, e,
                                                prev_states=hs_s if e == n_even - 1 else ())
            hs_s.append(h_s[0])
            hs_p.append(h_p)
            cbs_p.append(ct_p[:, 8 - (kc - 1):])
            cbs_s.append(ct_s[:, 8 - (kc - 1):])
            sbs_p.append(ut_p[:, 8 - (ks - 1):])
            sbs_s.append(ut_s[:, 8 - (ks - 1):])
            if xf is None:
                xf, xb = matmul_ln(y_p, w_out_b, None, xp2, *ln(0), alpha, x2=y_s, res2=xs2)
            else:
                xf, xb = matmul_ln(y_p, w_out_b, None, xf, *ln(0), alpha, x2=y_s)
        else:
            o = layer // 2
            xs3 = xf[MP:].reshape(BS, TS, D)
            pbs_p.append(last_rows(xf, BP, T, POOL_CTX))
            pbs_s.append(jnp.concatenate([state_pool[o][:, TS:], xs3], axis=1))
            p_s = pool_sample(state_pool, o, xs3, dg).reshape(MS, D)
            xf, xb = pool_ln(xf, p_s, w_pool_b[o], pool_scale[o][None, :], *ln(0), MP, T, alpha)

        q, wo_b = matmul(xb, wq_x, layer, BF16, side=(wo_x, layer))
        att = attn_prompt(q, mk2, mv2, layer, BP, T, mem_len, heads)
        att = attn_sample(att, cache_mem_k, cache_mem_v, layer, MP, TS)
        xf, xb = matmul_ln(att, wo_b, None, xf, *ln(1), alpha)

        hid, w_down_b = matmul(xb, w_up, layer, BF16, act="relu2", side=(w_down, layer))
        xf, xb = matmul_ln(hid, w_down_b, None, xf, *ln(2), alpha)

    return (xf[:MP].reshape(BP, T, D), xf[MP:].reshape(BS, TS, D),
            jnp.stack(hs_p), h_s, jnp.stack(cbs_p), jnp.stack(cbs_s),
            jnp.stack(sbs_p), jnp.stack(sbs_s), jnp.stack(pbs_p), jnp.stack(pbs_s), mk5, mv5)
```

```python
import functools
import math

import jax
import jax.numpy as jnp
from jax import lax
from jax.experimental import pallas as pl
from jax.experimental.pallas import tpu as pltpu

F32 = jnp.float32
BF16 = jnp.bfloat16
NT_DIMS = (((1,), (1,)), ((), ()))
TN_DIMS = (((0,), (0,)), ((), ()))

PAST_LEN = 16384
SSD_CHUNK = 128
POOL_WINDOWS = (2, 4, 8, 16)
POOL_CTX = 15
LN_EPS = 1e-5
RMS_EPS = 1e-5

V7X_LANES = 128
V7X_SUBLANES = 8
V7X_VMEM_BYTES = 64 * 1024 * 1024
VMEM_LIMIT = V7X_VMEM_BYTES - 8 * 1024 * 1024


def _tile(dim, pref, quantum):
    best = None
    t = quantum
    while t <= min(dim, pref):
        if dim % t == 0:
            best = t
        t += quantum
    return best if best is not None else dim


def _params(*sem):
    return pltpu.CompilerParams(dimension_semantics=sem, vmem_limit_bytes=VMEM_LIMIT)


def _silu(x):
    return x * jax.nn.sigmoid(x)


def _softplus(x):
    return jnp.maximum(x, 0.0) + jnp.log1p(jnp.exp(-jnp.abs(x)))


def _layer_norm(y, g, b):
    mu = jnp.mean(y, -1, keepdims=True)
    d = y - mu
    var = jnp.mean(d * d, -1, keepdims=True)
    return d * lax.rsqrt(var + LN_EPS) * g + b


def _split3(x):
    hi = x.astype(BF16)
    r = x - hi.astype(F32)
    mid = r.astype(BF16)
    lo = (r - mid.astype(F32)).astype(BF16)
    return hi, mid, lo


def _dot_sel(x, sel):
    hi, mid, lo = _split3(x)
    return (jnp.dot(hi, sel, preferred_element_type=F32) + jnp.dot(mid, sel, preferred_element_type=F32)
            + jnp.dot(lo, sel, preferred_element_type=F32))


def _sel_dot(sel, x):
    hi, mid, lo = _split3(x)
    return (jnp.dot(sel, hi, preferred_element_type=F32) + jnp.dot(sel, mid, preferred_element_type=F32)
            + jnp.dot(sel, lo, preferred_element_type=F32))


def _mm_kernel(*refs, act, transposed, n_side):
    x_ref, w_ref = refs[:2]
    o_ref = refs[2 + n_side]
    scratch = refs[3 + 2 * n_side:]
    for side_ref, side_out_ref in zip(refs[2:2 + n_side], refs[3 + n_side:3 + 2 * n_side]):
        side_out_ref[...] = side_ref[...].astype(BF16)
    if scratch:
        wb_ref, = scratch

        @pl.when(pl.program_id(1) == 0)
        def _():
            wb_ref[...] = (w_ref[0] if transposed else w_ref[...]).astype(BF16)

        w = wb_ref[...]
    else:
        w = w_ref[...]
    acc = lax.dot_general(x_ref[...], w, NT_DIMS if transposed else (((1,), (0,)), ((), ())),
                          preferred_element_type=F32)
    if act == "relu2":
        acc = jnp.square(jnp.maximum(acc, 0.0))
    o_ref[...] = acc.astype(o_ref.dtype)


def matmul(x, w, layer, out_dtype, act=None, transposed=False, col0=0, n_cols=None, sides=(), bm=1024, bn=1024):
    M, K = x.shape
    if transposed:
        assert w.dtype != BF16 and col0 % V7X_SUBLANES == 0
        N = n_cols
    else:
        assert col0 == 0 and n_cols is None
        N = w.shape[-1]
    bm = _tile(M, bm, V7X_SUBLANES)
    bn = _tile(N, bn, V7X_LANES)
    n_i = M // bm
    if transposed:
        w_spec = pl.BlockSpec((pl.Element(1), pl.Element(bn), pl.Element(K)),
                              lambda j, i: (layer, pl.multiple_of(col0 + j * bn, V7X_SUBLANES), 0))
    elif layer is None:
        w_spec = pl.BlockSpec((K, bn), lambda j, i: (0, j))
    else:
        w_spec = pl.BlockSpec((None, K, bn), lambda j, i: (layer, 0, j))
    in_specs = [pl.BlockSpec((bm, K), lambda j, i: (i, 0)), w_spec]
    out_specs = [pl.BlockSpec((bm, bn), lambda j, i: (i, j))]
    out_shape = [jax.ShapeDtypeStruct((M, N), out_dtype)]
    args = [x, w]
    n_steps = (N // bn) * n_i
    for s_arr, s_layer in sides:
        _, s_rows, s_cols = s_arr.shape
        rb = next(r for r in range(16, s_rows + 1, 16) if s_rows % r == 0 and s_rows // r <= n_steps)
        last = s_rows // rb - 1
        in_specs.append(pl.BlockSpec((None, rb, s_cols), functools.partial(
            lambda l, last, j, i: (l, jnp.minimum(j * n_i + i, last), 0), s_layer, last)))
        out_specs.append(pl.BlockSpec((rb, s_cols), functools.partial(
            lambda last, j, i: (jnp.minimum(j * n_i + i, last), 0), last)))
        out_shape.append(jax.ShapeDtypeStruct((s_rows, s_cols), BF16))
        args.append(s_arr)
    scratch = [pltpu.VMEM((bn, K) if transposed else (K, bn), BF16)] if w.dtype != BF16 else []
    outs = pl.pallas_call(
        functools.partial(_mm_kernel, act=act, transposed=transposed, n_side=len(sides)),
        grid=(N // bn, n_i),
        in_specs=in_specs,
        out_specs=out_specs,
        out_shape=out_shape,
        scratch_shapes=scratch,
        compiler_params=_params("arbitrary", "arbitrary"),
    )(*args)
    return outs[0] if not sides else outs


def _kv_proj_kernel(x_ref, w_ref, o2_ref, o5_ref, wb_ref, *, heads):
    @pl.when(pl.program_id(1) == 0)
    def _():
        wb_ref[...] = w_ref[...].astype(BF16)

    acc = jnp.dot(x_ref[...], wb_ref[...], preferred_element_type=F32)
    o2_ref[...] = acc
    hd = acc.shape[1] // heads
    for h in range(heads):
        o5_ref[0, :, h, :] = acc[:, h * hd:(h + 1) * hd]


def kv_proj(mem_b, w, nb, heads):
    M, K = mem_b.shape
    depth, _, N = w.shape
    mem_len = M // nb
    hd = N // heads
    return pl.pallas_call(
        functools.partial(_kv_proj_kernel, heads=heads),
        grid=(depth, nb),
        in_specs=[pl.BlockSpec((mem_len, K), lambda l, b: (b, 0)),
                  pl.BlockSpec((None, K, N), lambda l, b: (l, 0, 0), pipeline_mode=pl.Buffered(1))],
        out_specs=[pl.BlockSpec((None, mem_len, N), lambda l, b: (l, b, 0)),
                   pl.BlockSpec((None, 1, mem_len, heads, hd), lambda l, b: (l, b, 0, 0, 0))],
        out_shape=[jax.ShapeDtypeStruct((depth, M, N), F32),
                   jax.ShapeDtypeStruct((depth, nb, mem_len, heads, hd), F32)],
        scratch_shapes=[pltpu.VMEM((K, N), BF16)],
        compiler_params=_params("parallel", "arbitrary"),
    )(mem_b, w)


def _mm_ln_kernel(*refs, alpha, n_first, two_x, two_res):
    refs = list(refs)
    x_refs = [refs.pop(0) for _ in range(2 if two_x else 1)]
    w_ref = refs.pop(0)
    res_refs = [refs.pop(0) for _ in range(2 if two_res else 1)]
    g_ref, b_ref, of_ref, ob_ref = refs

    bm = of_ref.shape[0]
    n_chunks = 2 if bm >= 512 else 1
    ch = bm // n_chunks

    def body(src_ref, res_ref):
        for c in range(n_chunks):
            rows = slice(c * ch, (c + 1) * ch)
            m = jnp.dot(src_ref[rows, :], w_ref[...], preferred_element_type=F32)
            out = _layer_norm(alpha * res_ref[rows, :] + m, g_ref[...], b_ref[...])
            of_ref[rows, :] = out
            ob_ref[rows, :] = out.astype(BF16)

    if not (two_x or two_res):
        body(x_refs[0], res_refs[0])
    else:
        i = pl.program_id(0)

        @pl.when(i < n_first)
        def _():
            body(x_refs[0], res_refs[0])

        @pl.when(i >= n_first)
        def _():
            body(x_refs[-1], res_refs[-1])


def matmul_ln(x, w, layer, res, g, b, alpha, x2=None, res2=None):
    M1, K = x.shape
    M2 = 0 if x2 is None else x2.shape[0]
    M = M1 + M2
    N = w.shape[-1]
    assert res2 is None or (x2 is not None and res.shape[0] == M1 and res2.shape[0] == M2)
    n_x = 1 if x2 is None else 2
    n_res = 1 if res2 is None else 2
    w_bytes = K * N * 2
    per_row = n_x * 2 * (K * 2) + n_res * 2 * (N * 4) + 2 * (N * 4) + 2 * (N * 2) + 2 * (N * 4)
    bm = _tile(math.gcd(M1, M2) if M2 else M1, min(512, (VMEM_LIMIT - w_bytes) // per_row), 2 * V7X_SUBLANES)
    n_first = M1 // bm

    def row_specs(width, two):
        if not two:
            return [pl.BlockSpec((bm, width), lambda i: (i, 0))]
        return [pl.BlockSpec((bm, width), lambda i: (jnp.minimum(i, n_first - 1), 0)),
                pl.BlockSpec((bm, width), lambda i: (jnp.maximum(i - n_first, 0), 0))]

    if layer is None:
        w_spec = pl.BlockSpec((K, N), lambda i: (0, 0), pipeline_mode=pl.Buffered(1))
    else:
        w_spec = pl.BlockSpec((None, K, N), lambda i: (layer, 0, 0), pipeline_mode=pl.Buffered(1))
    return pl.pallas_call(
        functools.partial(_mm_ln_kernel, alpha=alpha, n_first=n_first, two_x=x2 is not None,
                          two_res=res2 is not None),
        grid=(M // bm,),
        in_specs=row_specs(K, x2 is not None) + [w_spec] + row_specs(N, res2 is not None) + [
            pl.BlockSpec((1, N), lambda i: (0, 0)),
            pl.BlockSpec((1, N), lambda i: (0, 0))],
        out_specs=[pl.BlockSpec((bm, N), lambda i: (i, 0)),
                   pl.BlockSpec((bm, N), lambda i: (i, 0))],
        out_shape=[jax.ShapeDtypeStruct((M, N), F32), jax.ShapeDtypeStruct((M, N), BF16)],
        compiler_params=_params("arbitrary"),
    )(*([x] if x2 is None else [x, x2]), w, *([res] if res2 is None else [res, res2]), g, b)


def _prompt_mixer_kernel(z_ref, xs_ref, bc_ref, gb_ref, gc_ref, hh_ref, dt_ref, convw_ref, convb_ref, dtb_ref,
                         alog_ref, dskip_ref, normg_ref, scw_ref, tril_ref, expand_ref,
                         y_ref, hout_ref, utail_ref, ctail_ref, state_ref, ext_ref, act_ref, extu_ref,
                         *, L, nc, G, R, P, N):
    RP = R * P
    D = G * RP
    GN = G * N
    CH = D + 2 * GN
    KC = convw_ref.shape[0]
    KS = scw_ref.shape[0]
    c = pl.program_id(1)

    @pl.when(c == 0)
    def _init():
        ext_ref[0:8, :] = jnp.zeros((8, CH), F32)
        extu_ref[0:8, :] = jnp.zeros((8, D), F32)
        state_ref[...] = jnp.zeros(state_ref.shape, F32)

    ext_ref[8:8 + L, 0:D] = xs_ref[...]
    ext_ref[8:8 + L, D:CH] = bc_ref[...]
    cw = min(512, CH)
    for j in range(CH // cw):
        cols = slice(j * cw, (j + 1) * cw)
        base = 8 - (KC - 1)
        acc = ext_ref[base:base + L, cols] * convw_ref[0:1, cols]
        for k in range(1, KC):
            acc = acc + ext_ref[base + k:base + k + L, cols] * convw_ref[k:k + 1, cols]
        act_ref[:, cols] = _silu(acc + convb_ref[:, cols])

    dt = _softplus(dt_ref[...] + dtb_ref[...])
    a = dt * (-jnp.exp(alog_ref[...]))
    acs = _sel_dot(tril_ref[...], a)
    dt_e = _dot_sel(dt, expand_ref[...])
    acs_e = _dot_sel(acs, expand_ref[...])
    acs_row = acs.T

    li = lax.broadcasted_iota(jnp.int32, (L, L), 0)
    si = lax.broadcasted_iota(jnp.int32, (L, L), 1)
    causal = si <= li
    ci = lax.broadcasted_iota(jnp.int32, (L, RP), 1)
    head_cols = [jnp.logical_and(ci >= r * P, ci < (r + 1) * P) for r in range(R)]

    for g in range(G):
        gc_ = slice(g * RP, (g + 1) * RP)
        xs_g = act_ref[:, gc_]
        b_g = act_ref[:, D + g * N:D + (g + 1) * N]
        c_g = act_ref[:, D + GN + g * N:D + GN + (g + 1) * N].astype(BF16)
        xdt = xs_g * dt_e[:, gc_]
        x_bd = jnp.concatenate([jnp.where(head_cols[r], xdt, 0.0) for r in range(R)], axis=0).astype(BF16)
        acs_g = acs_e[:, gc_]
        bt_g = b_g.T
        cb = jnp.dot(c_g, bt_g.astype(BF16), preferred_element_type=F32)

        ms, dbs = [], []
        for r in range(R):
            h = g * R + r
            col = acs[:, h:h + 1]
            row = acs_row[h:h + 1, :]
            ms.append(cb * jnp.exp(jnp.where(causal, col - row, -jnp.inf)))
            dbs.append(bt_g * jnp.exp(acs_row[h:h + 1, L - 1:L] - row))

        y = jnp.dot(jnp.concatenate(ms, axis=1).astype(BF16), x_bd, preferred_element_type=F32)
        h_prev = state_ref[g]
        y_off = jnp.dot(c_g, h_prev.astype(BF16), preferred_element_type=F32)
        dbt = jnp.concatenate(dbs, axis=1).astype(BF16)
        new_states = jnp.dot(dbt, x_bd, preferred_element_type=F32)
        state_ref[g] = h_prev * jnp.exp(acs_g[L - 1:L, :]) + new_states
        y = y + y_off * jnp.exp(acs_g)
        y = y + dskip_ref[:, gc_] * xs_g
        y = y * _silu(z_ref[:, gc_])
        y = y * lax.rsqrt(jnp.mean(y * y, -1, keepdims=True) + RMS_EPS)
        y_ref[:, gc_] = (y * normg_ref[:, gc_]).astype(BF16)

    cw = min(512, D)
    for j in range(D // cw):
        cols = slice(j * cw, (j + 1) * cw)
        extu_ref[8:8 + L, cols] = gc_ref[:, cols] * hh_ref[:, cols]
        base = 8 - (KS - 1)
        v = extu_ref[base:base + L, cols] * scw_ref[0:1, cols]
        for k in range(1, KS):
            v = v + extu_ref[base + k:base + k + L, cols] * scw_ref[k:k + 1, cols]
        y_ref[:, D + j * cw:D + (j + 1) * cw] = (gb_ref[:, cols] * v).astype(BF16)

    @pl.when(c == nc - 1)
    def _fin():
        utail_ref[0] = extu_ref[L:L + 8, :]
        ctail_ref[0] = ext_ref[L:L + 8, :]
        for g in range(G):
            hout_ref[0, g * R:(g + 1) * R] = state_ref[g].T.reshape(R, P, N)

    ext_ref[0:8, :] = ext_ref[L:L + 8, :]
    extu_ref[0:8, :] = extu_ref[L:L + 8, :]


def _mixer_consts(D, P):
    expand = (jnp.arange(V7X_LANES)[:, None] == (jnp.arange(D) // P)[None, :]).astype(BF16)
    return expand


def _proj_specs(rows, D, row_block):
    return [pl.BlockSpec((rows, D), functools.partial(lambda j, *idx: (row_block(*idx), j), j)) for j in (0, 1, 2)] * 2


def prompt_mixer(proj_a, proj_b, dt_raw, nb, T, cfg, wts):
    D, G, R, P, N = cfg["D"], cfg["G"], cfg["R"], cfg["P"], cfg["N"]
    H = G * R
    GN = G * N
    CH = D + 2 * GN
    L = SSD_CHUNK
    nc = T // L
    assert T % L == 0 and L == V7X_LANES and 2 * GN == D and R * P * G == D
    tril = jnp.tril(jnp.ones((L, L), BF16))
    expand = _mixer_consts(D, P)
    consts = list(wts) + [tril, expand]

    def row_block(b, c):
        return b * nc + c

    in_specs = (_proj_specs(L, D, row_block) + [pl.BlockSpec((L, V7X_LANES), lambda b, c: (b * nc + c, 0))]
                + [pl.BlockSpec(a.shape, functools.partial(lambda n, b, c: (0,) * n, a.ndim)) for a in consts])
    return pl.pallas_call(
        functools.partial(_prompt_mixer_kernel, L=L, nc=nc, G=G, R=R, P=P, N=N),
        grid=(nb, nc),
        in_specs=in_specs,
        out_specs=[pl.BlockSpec((L, 2 * D), lambda b, c: (b * nc + c, 0)),
                   pl.BlockSpec((1, H, P, N), lambda b, c: (b, 0, 0, 0)),
                   pl.BlockSpec((1, 8, D), lambda b, c: (b, 0, 0)),
                   pl.BlockSpec((1, 8, CH), lambda b, c: (b, 0, 0))],
        out_shape=[jax.ShapeDtypeStruct((nb * T, 2 * D), BF16),
                   jax.ShapeDtypeStruct((nb, H, P, N), F32),
                   jax.ShapeDtypeStruct((nb, 8, D), F32),
                   jax.ShapeDtypeStruct((nb, 8, CH), F32)],
        scratch_shapes=[pltpu.VMEM((G, N, R * P), F32), pltpu.VMEM((L + 8, CH), F32),
                        pltpu.VMEM((L, CH), F32), pltpu.VMEM((L + 8, D), F32)],
        compiler_params=_params("parallel", "arbitrary"),
    )(*([proj_a] * 3 + [proj_b] * 3 + [dt_raw] + consts))


def _sample_mixer_kernel(*refs, bb, T, G, R, P, N, n_prev):
    (z_ref, xs_ref, bc_ref, gb_ref, gc_ref, hh_ref, dt_ref, convw_ref, convb_ref, dtb_ref, alog_ref,
     dskip_ref, normg_ref, scw_ref, btril_ref, expand_ref, expand_t_ref, rep_ref,
     h0_ref, ctx_ref, scctx_ref) = refs[:21]
    prev_refs = refs[21:21 + n_prev]
    y_ref, hout_ref, utail_ref, ctail_ref, ext_ref, act_ref, extu_ref = refs[21 + n_prev:]
    H = G * R
    RP = R * P
    D = G * RP
    GN = G * N
    CH = D + 2 * GN
    HT = H * T
    KC = convw_ref.shape[0]
    KS = scw_ref.shape[0]
    rows = bb * T

    ext_ref[:, 8 - (KC - 1):8, :] = ctx_ref[...]
    ext_ref[:, 8:8 + T, 0:D] = xs_ref[...].reshape(bb, T, D)
    ext_ref[:, 8:8 + T, D:CH] = bc_ref[...].reshape(bb, T, 2 * GN)
    ctail_ref[...] = ext_ref[:, T:T + 8, :]
    cw = min(512, CH)
    for j in range(CH // cw):
        cols = slice(j * cw, (j + 1) * cw)
        base = 8 - (KC - 1)
        acc = ext_ref[:, base:base + T, cols] * convw_ref[0:1, cols]
        for k in range(1, KC):
            acc = acc + ext_ref[:, base + k:base + k + T, cols] * convw_ref[k:k + 1, cols]
        act_ref[:, cols] = _silu(acc + convb_ref[:, cols]).reshape(rows, cw)

    dt = _softplus(dt_ref[...] + dtb_ref[...])
    a = dt * (-jnp.exp(alog_ref[...]))
    acs = _sel_dot(btril_ref[...], a)
    dt_e = _dot_sel(dt, expand_ref[...])
    acs_e = _dot_sel(acs, expand_ref[...])
    acs_e3 = acs_e.reshape(bb, T, D)
    last_e = acs_e3[:, T - 1:T, :]
    xs_act = act_ref[:, 0:D]
    xdt = xs_act * dt_e
    xde = xdt * jnp.exp(last_e - acs_e3).reshape(rows, D)
    a1 = _dot_sel(acs, expand_t_ref[...]).reshape(bb, T, HT)
    l3 = lax.broadcasted_iota(jnp.int32, (bb, T, HT), 1)
    s3 = lax.broadcasted_iota(jnp.int32, (bb, T, HT), 2) & (T - 1)
    a2 = jnp.sum(jnp.where(l3 == s3, a1, 0.0), axis=1, keepdims=True)
    decay = jnp.exp(jnp.where(s3 <= l3, a1 - a2, -jnp.inf)).reshape(rows, HT)
    cd_heads = jnp.exp(acs)

    ri = lax.broadcasted_iota(jnp.int32, (HT, D), 0)
    ci = lax.broadcasted_iota(jnp.int32, (HT, D), 1)
    bd_mask = (ri >> (T.bit_length() - 1)) == (ci >> (P.bit_length() - 1))
    lane_g = lax.broadcasted_iota(jnp.int32, (T, HT), 1) >> ((R * T).bit_length() - 1)

    y_diag, y_off = [], []
    for b in range(bb):
        rs = slice(b * T, (b + 1) * T)
        b_pieces = [act_ref[rs, D + g * N:D + (g + 1) * N] for g in range(G)]
        c_st = jnp.concatenate([act_ref[rs, D + GN + g * N:D + GN + (g + 1) * N] for g in range(G)],
                               axis=0).astype(BF16)
        b_st = jnp.concatenate(b_pieces, axis=0).astype(BF16)
        call = lax.dot_general(c_st, b_st, NT_DIMS, preferred_element_type=F32)
        q = _dot_sel(call, rep_ref[...])
        cb = jnp.zeros((T, HT), F32)
        for g in range(G):
            cb = cb + jnp.where(lane_g == g, q[g * T:(g + 1) * T], 0.0)
        mcat = (cb * decay[rs]).astype(BF16)
        x_bd = jnp.where(bd_mask, jnp.concatenate([xdt[rs]] * H, axis=0), 0.0).astype(BF16)
        y_diag.append(jnp.dot(mcat, x_bd, preferred_element_type=F32))

        h0_b = h0_ref[b].reshape(D, N)
        yy = lax.dot_general(c_st, h0_b.astype(BF16), NT_DIMS, preferred_element_type=F32)
        y_off.append(jnp.concatenate([yy[g * T:(g + 1) * T, g * RP:(g + 1) * RP] for g in range(G)], axis=1))

        x_bd2 = jnp.where(bd_mask, jnp.concatenate([xde[rs]] * H, axis=0), 0.0).astype(BF16)
        b_rep = jnp.concatenate([b_pieces[h // R] for h in range(H)], axis=0).astype(BF16)
        new_states = lax.dot_general(x_bd2, b_rep, TN_DIMS, preferred_element_type=F32)
        r_last = b * T + T - 1
        cd = jnp.concatenate([jnp.broadcast_to(cd_heads[r_last:r_last + 1, h:h + 1], (P, N)) for h in range(H)],
                             axis=0)
        hout_ref[n_prev, b] = (h0_b * cd + new_states).reshape(H, P, N)
    for s, prev_ref in enumerate(prev_refs):
        hout_ref[s] = prev_ref[...]

    y = jnp.concatenate(y_diag, axis=0) + jnp.concatenate(y_off, axis=0) * jnp.exp(acs_e)
    y = y + dskip_ref[...] * xs_act
    y = y * _silu(z_ref[...])
    for g in range(G):
        gc_ = slice(g * RP, (g + 1) * RP)
        yg = y[:, gc_]
        yg = yg * lax.rsqrt(jnp.mean(yg * yg, -1, keepdims=True) + RMS_EPS)
        y_ref[:, gc_] = (yg * normg_ref[:, gc_]).astype(BF16)

    extu_ref[:, 8 - (KS - 1):8, :] = scctx_ref[...]
    cw = min(512, D)
    for j in range(D // cw):
        cols = slice(j * cw, (j + 1) * cw)
        extu_ref[:, 8:8 + T, cols] = (gc_ref[:, cols] * hh_ref[:, cols]).reshape(bb, T, cw)
        base = 8 - (KS - 1)
        v = extu_ref[:, base:base + T, cols] * scw_ref[0:1, cols]
        for k in range(1, KS):
            v = v + extu_ref[:, base + k:base + k + T, cols] * scw_ref[k:k + 1, cols]
        y_ref[:, D + j * cw:D + (j + 1) * cw] = (gb_ref[:, cols] * v.reshape(rows, cw)).astype(BF16)
    utail_ref[...] = extu_ref[:, T:T + 8, :]


def sample_mixer(proj_a, proj_b, dt_raw, row0, nb, T, cfg, wts, state_ssd, state_ssd_conv, state_short_conv,
                 layer_e, prev_states=(), bb=4):
    D, G, R, P, N = cfg["D"], cfg["G"], cfg["R"], cfg["P"], cfg["N"]
    H = G * R
    GN = G * N
    CH = D + 2 * GN
    bb = _tile(nb, bb, 2)
    rows = bb * T
    pow2 = lambda v: v & (v - 1) == 0
    assert T == V7X_SUBLANES and row0 % rows == 0 and 2 * GN == D and R * P * G == D
    assert pow2(P) and pow2(R)
    rb0 = row0 // rows
    kc1, ks1 = state_ssd_conv.shape[2], state_short_conv.shape[2]
    assert kc1 <= T and ks1 <= T
    seq = jnp.arange(rows) // T
    btril = jnp.logical_and(seq[:, None] == seq[None, :], jnp.arange(rows)[:, None] >= jnp.arange(rows)[None, :])
    expand = _mixer_consts(D, P)
    lane_h = jnp.arange(H * T) // T
    expand_t = (jnp.arange(V7X_LANES)[:, None] == lane_h[None, :]).astype(BF16)
    rep = jnp.logical_and((jnp.arange(G * T) // T)[:, None] == (lane_h // R)[None, :],
                          (jnp.arange(G * T) % T)[:, None] == (jnp.arange(H * T) % T)[None, :]).astype(BF16)
    consts = list(wts) + [btril.astype(BF16), expand, expand_t, rep]
    n_prev = len(prev_states)

    in_specs = (_proj_specs(rows, D, lambda i: rb0 + i) + [pl.BlockSpec((rows, V7X_LANES), lambda i: (rb0 + i, 0))]
                + [pl.BlockSpec(a.shape, functools.partial(lambda n, i: (0,) * n, a.ndim)) for a in consts]
                + [pl.BlockSpec((None, bb, H, P, N), lambda i: (layer_e, i, 0, 0, 0)),
                   pl.BlockSpec((None, bb, kc1, CH), lambda i: (layer_e, i, 0, 0)),
                   pl.BlockSpec((None, bb, ks1, D), lambda i: (layer_e, i, 0, 0))]
                + [pl.BlockSpec((bb, H, P, N), lambda i: (i, 0, 0, 0))] * n_prev)
    return pl.pallas_call(
        functools.partial(_sample_mixer_kernel, bb=bb, T=T, G=G, R=R, P=P, N=N, n_prev=n_prev),
        grid=(nb // bb,),
        in_specs=in_specs,
        out_specs=[pl.BlockSpec((rows, 2 * D), lambda i: (i, 0)),
                   pl.BlockSpec((n_prev + 1, bb, H, P, N), lambda i: (0, i, 0, 0, 0)),
                   pl.BlockSpec((bb, 8, D), lambda i: (i, 0, 0)),
                   pl.BlockSpec((bb, 8, CH), lambda i: (i, 0, 0))],
        out_shape=[jax.ShapeDtypeStruct((nb * T, 2 * D), BF16),
                   jax.ShapeDtypeStruct((n_prev + 1, nb, H, P, N), F32),
                   jax.ShapeDtypeStruct((nb, 8, D), F32),
                   jax.ShapeDtypeStruct((nb, 8, CH), F32)],
        scratch_shapes=[pltpu.VMEM((bb, 8 + T, CH), F32), pltpu.VMEM((rows, CH), F32),
                        pltpu.VMEM((bb, 8 + T, D), F32)],
        compiler_params=_params("parallel"),
    )(*([proj_a] * 3 + [proj_b] * 3 + [dt_raw] + consts + [state_ssd, state_ssd_conv, state_short_conv]
        + list(prev_states)))


def _pool_sample_kernel(st_ref, x_ref, p_ref, ext_ref, *, start_pos, dg):
    T = x_ref.shape[1]
    ext_ref[:, 16 - POOL_CTX:16, :] = st_ref[...]
    ext_ref[:, 16:16 + T, :] = x_ref[...]
    pos = start_pos + lax.broadcasted_iota(jnp.int32, (1, T, 1), 1)
    for gi, w in enumerate(POOL_WINDOWS):
        cols = slice(gi * dg, (gi + 1) * dg)
        s = ext_ref[:, 16:16 + T, cols]
        for j in range(1, w):
            s = s + ext_ref[:, 16 - j:16 - j + T, cols]
        cnt = jnp.minimum(pos + 1, w).astype(F32)
        p_ref[:, :, cols] = (s / cnt - x_ref[:, :, cols]).astype(BF16)


def pool_sample(state_pool, layer_o, x3, dg):
    B, T, D = x3.shape
    bb = _tile(B, 16, 1)
    return pl.pallas_call(
        functools.partial(_pool_sample_kernel, start_pos=PAST_LEN, dg=dg),
        grid=(B // bb,),
        in_specs=[pl.BlockSpec((None, bb, POOL_CTX, D), lambda i: (layer_o, i, 0, 0)),
                  pl.BlockSpec((bb, T, D), lambda i: (i, 0, 0))],
        out_specs=pl.BlockSpec((bb, T, D), lambda i: (i, 0, 0)),
        out_shape=jax.ShapeDtypeStruct((B, T, D), BF16),
        scratch_shapes=[pltpu.VMEM((bb, 16 + T, D), F32)],
        compiler_params=_params("parallel"),
    )(state_pool, x3)


POOL_TOP = 8 + 16


def _pool_ln_kernel(x_ref, halo_ref, ps_ref, w_ref, scale_ref, g_ref, b_ref, of_ref, ob_ref, ext_ref, tmp_ref, p_ref,
                    *, n_prompt_tiles, tiles_per_seq, dg, alpha):
    i = pl.program_id(0)
    bm = x_ref.shape[0]
    top = POOL_TOP
    end = top + bm

    @pl.when(i < n_prompt_tiles)
    def _prompt():
        first = (i % tiles_per_seq) == 0
        ext_ref[0:8, :] = jnp.zeros((8, ext_ref.shape[1]), F32)
        ext_ref[8:top, :] = jnp.where(first, 0.0, halo_ref[...])
        ext_ref[top:end, :] = x_ref[...]
        pos = (i % tiles_per_seq) * bm + lax.broadcasted_iota(jnp.int32, (bm, 1), 0)
        for gi, w in enumerate(POOL_WINDOWS):
            cols = slice(gi * dg, (gi + 1) * dg)
            src, src_cols, k, level = ext_ref, cols, 1, 0
            while 2 * k < w:
                dst = tmp_ref.at[level % 2]
                dst[0:8, :] = jnp.zeros((8, dg), F32)
                dst[8:end, :] = src[8:end, src_cols] + src[8 - k:end - k, src_cols]
                src, src_cols, k, level = dst, slice(None), 2 * k, level + 1
            s = src[top:end, src_cols] + src[top - k:end - k, src_cols]
            cnt = jnp.minimum(pos + 1, w).astype(F32)
            p_ref[:, cols] = (s / cnt - x_ref[:, cols]).astype(BF16)

    @pl.when(i >= n_prompt_tiles)
    def _sample():
        p_ref[...] = ps_ref[...]

    for gi in range(len(POOL_WINDOWS)):
        cols = slice(gi * dg, (gi + 1) * dg)
        m = jnp.dot(p_ref[:, cols], w_ref[gi], preferred_element_type=F32) * scale_ref[:, cols]
        ext_ref[top:end, cols] = alpha * x_ref[:, cols] + m
    out = _layer_norm(ext_ref[top:end, :], g_ref[...], b_ref[...])
    of_ref[...] = out
    ob_ref[...] = out.astype(BF16)


def pool_ln(x, p_sample, w_pool, scale, g, b, n_prompt_rows, T, alpha, bm=512):
    M, D = x.shape
    dg = D // len(POOL_WINDOWS)
    bm = _tile(math.gcd(math.gcd(n_prompt_rows, M - n_prompt_rows), T), bm, 16)
    npt = n_prompt_rows // bm
    hb = bm // 16
    return pl.pallas_call(
        functools.partial(_pool_ln_kernel, n_prompt_tiles=npt, tiles_per_seq=T // bm, dg=dg, alpha=alpha),
        grid=(M // bm,),
        in_specs=[pl.BlockSpec((bm, D), lambda i: (i, 0)),
                  pl.BlockSpec((16, D), lambda i: (jnp.maximum(i * hb - 1, 0), 0)),
                  pl.BlockSpec((bm, D), lambda i: (jnp.maximum(i - npt, 0), 0)),
                  pl.BlockSpec(w_pool.shape, lambda i: (0, 0, 0)),
                  pl.BlockSpec((1, D), lambda i: (0, 0)),
                  pl.BlockSpec((1, D), lambda i: (0, 0)),
                  pl.BlockSpec((1, D), lambda i: (0, 0))],
        out_specs=[pl.BlockSpec((bm, D), lambda i: (i, 0)), pl.BlockSpec((bm, D), lambda i: (i, 0))],
        out_shape=[jax.ShapeDtypeStruct((M, D), F32), jax.ShapeDtypeStruct((M, D), BF16)],
        scratch_shapes=[pltpu.VMEM((POOL_TOP + bm, D), F32), pltpu.VMEM((2, POOL_TOP + bm, dg), F32),
                        pltpu.VMEM((bm, D), BF16)],
        compiler_params=_params("arbitrary"),
    )(x, x, p_sample, w_pool, scale, g, b)


def _softmax_rows(s):
    m = jnp.max(s, -1, keepdims=True)
    e = jnp.exp(s - m)
    return e / jnp.sum(e, -1, keepdims=True)


def _attn_prompt_kernel(q_ref, k_ref, v_ref, o_ref, *, scale):
    s = lax.dot_general(q_ref[...], k_ref[...].astype(BF16), (((1,), (1,)), ((), ())),
                        preferred_element_type=F32) * scale
    p = _softmax_rows(s).astype(BF16)
    o_ref[...] = jnp.dot(p, v_ref[...].astype(BF16), preferred_element_type=F32).astype(BF16)


def attn_prompt(q, mk, mv, layer, nb, T, mem_len, heads, tq=2048):
    M, D = q.shape
    hd = D // heads
    tq = _tile(T, tq, V7X_SUBLANES)
    nt = T // tq
    return pl.pallas_call(
        functools.partial(_attn_prompt_kernel, scale=hd ** -0.5),
        grid=(nb, heads, nt),
        in_specs=[pl.BlockSpec((tq, hd), lambda b, h, t: (b * nt + t, h)),
                  pl.BlockSpec((None, mem_len, hd), lambda b, h, t: (layer, b, h)),
                  pl.BlockSpec((None, mem_len, hd), lambda b, h, t: (layer, b, h))],
        out_specs=pl.BlockSpec((tq, hd), lambda b, h, t: (b * nt + t, h)),
        out_shape=jax.ShapeDtypeStruct((M, D), BF16),
        input_output_aliases={0: 0},
        compiler_params=_params("parallel", "parallel", "parallel"),
    )(q, mk, mv)


def _attn_sample_kernel(q_ref, k_hbm, v_hbm, o_ref, kbuf, vbuf, sem, *, layer, scale, T):
    _, heads, bb, mem_len, hd = kbuf.shape
    i = pl.program_id(0)
    n = pl.num_programs(0)

    def copies(step, slot):
        cps = []
        for h in range(heads):
            for t, (src, dst) in enumerate(((k_hbm, kbuf), (v_hbm, vbuf))):
                cps.append(pltpu.make_async_copy(src.at[layer, pl.ds(step * bb, bb), :, h, :],
                                                 dst.at[slot, h], sem.at[slot, t, h]))
        return cps

    @pl.when(i == 0)
    def _():
        for cp in copies(0, 0):
            cp.start()

    slot = i % 2

    @pl.when(i + 1 < n)
    def _():
        for cp in copies(i + 1, 1 - slot):
            cp.start()

    for cp in copies(i, slot):
        cp.wait()

    slots = bb * heads
    q_all = q_ref[...].astype(F32)
    st = jnp.zeros((mem_len, slots * T), F32)
    for b in range(bb):
        for h in range(heads):
            s_ = b * heads + h
            q_bh = q_all[b * T:(b + 1) * T, h * hd:(h + 1) * hd]
            parts = []
            if s_:
                parts.append(jnp.zeros((s_ * T, hd), F32))
            parts.append(q_bh)
            if s_ + 1 < slots:
                parts.append(jnp.zeros(((slots - s_ - 1) * T, hd), F32))
            q_slot = jnp.concatenate(parts, axis=0).astype(BF16)
            st = st + lax.dot_general(kbuf[slot, h, b].astype(BF16), q_slot, (((1,), (1,)), ((), ())),
                                      preferred_element_type=F32)
    st = st * scale
    e = jnp.exp(st - jnp.max(st, 0, keepdims=True))
    p_all = (e / jnp.sum(e, 0, keepdims=True)).T
    rows = []
    for b in range(bb):
        outs = []
        for h in range(heads):
            s_ = b * heads + h
            p = p_all[s_ * T:(s_ + 1) * T, :].astype(BF16)
            outs.append(jnp.dot(p, vbuf[slot, h, b].astype(BF16), preferred_element_type=F32))
        rows.append(jnp.concatenate(outs, axis=1))
    o_ref[...] = jnp.concatenate(rows, axis=0).astype(BF16)


def attn_sample(q, cache_k, cache_v, layer, row0, T):
    _, B, mem_len, heads, hd = cache_k.shape
    M, D = q.shape
    bb = V7X_LANES // (heads * T)
    assert bb * heads * T == V7X_LANES and B % bb == 0
    rows = bb * T
    assert row0 % rows == 0 and rows % 16 == 0
    rb0 = row0 // rows
    return pl.pallas_call(
        functools.partial(_attn_sample_kernel, layer=layer, scale=hd ** -0.5, T=T),
        grid=(B // bb,),
        in_specs=[pl.BlockSpec((rows, D), lambda i: (rb0 + i, 0)),
                  pl.BlockSpec(memory_space=pl.ANY),
                  pl.BlockSpec(memory_space=pl.ANY)],
        out_specs=pl.BlockSpec((rows, D), lambda i: (rb0 + i, 0)),
        out_shape=jax.ShapeDtypeStruct((M, D), BF16),
        input_output_aliases={0: 0},
        scratch_shapes=[pltpu.VMEM((2, heads, bb, mem_len, hd), F32),
                        pltpu.VMEM((2, heads, bb, mem_len, hd), F32),
                        pltpu.SemaphoreType.DMA((2, 2, heads))],
        compiler_params=_params("arbitrary"),
    )(q, cache_k, cache_v)


def kernel(x_prompt, x_sample, state_ssd, state_ssd_conv, state_short_conv, state_pool, cache_mem_k, cache_mem_v, mem_prompt, w_in_even, ssd_conv_w, ssd_conv_b, ssd_dt_bias, ssd_a_log, ssd_d, ssd_norm_g, sc_conv_w, w_out_even, w_pool, pool_scale, wq_x, wk_x, wv_x, wo_x, w_up, w_down, ln_g, ln_b):
    BP, T, D = x_prompt.shape
    BS, TS, _ = x_sample.shape
    depth = wq_x.shape[0]
    n_even = w_in_even.shape[0]
    _, _, H, P, N = state_ssd.shape
    CH = state_ssd_conv.shape[-1]
    G = (CH - D) // (2 * N)
    R = H // G
    cfg = dict(D=D, G=G, R=R, P=P, N=N)
    mem_len, heads, hd = cache_mem_k.shape[2:]
    dg = w_pool.shape[-1]
    alpha = (2.0 * depth) ** 0.25
    MP, MS = BP * T, BS * TS
    o_dt = D + CH
    assert ssd_d.shape[-1] == H and H <= V7X_LANES and heads * hd == D

    xp2, xs2 = x_prompt.reshape(MP, D), x_sample.reshape(MS, D)
    xb = jnp.concatenate([xp2.astype(BF16), xs2.astype(BF16)], axis=0)
    xf = None
    mem_b = mem_prompt.reshape(BP * mem_len, D).astype(BF16)
    mk2, mk5 = kv_proj(mem_b, wk_x, BP, heads)
    mv2, mv5 = kv_proj(mem_b, wv_x, BP, heads)

    w_in_t = jnp.swapaxes(w_in_even, 1, 2)
    w_dt = jnp.pad(w_in_even[:, :, o_dt:o_dt + H], ((0, 0), (0, 0), (0, V7X_LANES - H))).astype(BF16)
    w_pool_b = w_pool.astype(BF16)

    def pad_lanes(v):
        return jnp.pad(v.astype(F32), (0, V7X_LANES - v.shape[0]))[None, :]

    def last_rows(x2, nb, t, n):
        return jnp.stack([x2[(b + 1) * t - n:(b + 1) * t] for b in range(nb)])

    kc, ks = ssd_conv_w.shape[1], sc_conv_w.shape[1]
    hs_p, hs_s, cbs_p, cbs_s, sbs_p, sbs_s, pbs_p, pbs_s = [], [], [], [], [], [], [], []
    h_s = None
    for layer in range(depth):
        ln = lambda j: (ln_g[layer, j][None, :], ln_b[layer, j][None, :])
        if layer % 2 == 0:
            e = layer // 2
            proj_a, w_out_b = matmul(xb, w_in_t, e, F32, transposed=True, col0=0, n_cols=o_dt,
                                     sides=[(w_out_even, e)])
            proj_b = matmul(xb, w_in_t, e, F32, transposed=True, col0=o_dt + H, n_cols=3 * D,
                            sides=[(w_up, 0)] if layer == 0 else [])
            if layer == 0:
                proj_b, w_up_b = proj_b
            dt_raw = matmul(xb, w_dt, e, F32)
            wts = (ssd_conv_w[e], ssd_conv_b[e][None, :], pad_lanes(ssd_dt_bias[e]), pad_lanes(ssd_a_log[e]),
                   jnp.repeat(ssd_d[e].astype(F32), P)[None, :], ssd_norm_g[e][None, :], sc_conv_w[e])
            y_p, h_p, ut_p, ct_p = prompt_mixer(proj_a, proj_b, dt_raw, BP, T, cfg, wts)
            y_s, h_s, ut_s, ct_s = sample_mixer(proj_a, proj_b, dt_raw, MP, BS, TS, cfg, wts,
                                                state_ssd, state_ssd_conv, state_short_conv, e,
                                                prev_states=hs_s if e == n_even - 1 else ())
            hs_s.append(h_s[0])
            hs_p.append(h_p)
            cbs_p.append(ct_p[:, 8 - (kc - 1):])
            cbs_s.append(ct_s[:, 8 - (kc - 1):])
            sbs_p.append(ut_p[:, 8 - (ks - 1):])
            sbs_s.append(ut_s[:, 8 - (ks - 1):])
            if xf is None:
                xf, xb = matmul_ln(y_p, w_out_b, None, xp2, *ln(0), alpha, x2=y_s, res2=xs2)
            else:
                xf, xb = matmul_ln(y_p, w_out_b, None, xf, *ln(0), alpha, x2=y_s)
        else:
            o = layer // 2
            xs3 = xf[MP:].reshape(BS, TS, D)
            pbs_p.append(last_rows(xf, BP, T, POOL_CTX))
            pbs_s.append(jnp.concatenate([state_pool[o][:, TS:], xs3], axis=1))
            p_s = pool_sample(state_pool, o, xs3, dg).reshape(MS, D)
            xf, xb = pool_ln(xf, p_s, w_pool_b[o], pool_scale[o][None, :], *ln(0), MP, T, alpha)

        q, wo_b = matmul(xb, wq_x, layer, BF16, sides=[(wo_x, layer)])
        att = attn_prompt(q, mk2, mv2, layer, BP, T, mem_len, heads)
        att = attn_sample(att, cache_mem_k, cache_mem_v, layer, MP, TS)
        xf, xb = matmul_ln(att, wo_b, None, xf, *ln(1), alpha)

        outs = matmul(xb, w_up_b, None, BF16, act="relu2",
                      sides=[(w_down, layer)] + ([(w_up, layer + 1)] if layer + 1 < depth else []))
        hid, w_down_b = outs[0], outs[1]
        w_up_b = outs[2] if layer + 1 < depth else None
        xf, xb = matmul_ln(hid, w_down_b, None, xf, *ln(2), alpha)

    return (xf[:MP].reshape(BP, T, D), xf[MP:].reshape(BS, TS, D),
            jnp.stack(hs_p), h_s, jnp.stack(cbs_p), jnp.stack(cbs_s),
            jnp.stack(sbs_p), jnp.stack(sbs_s), jnp.stack(pbs_p), jnp.stack(pbs_s), mk5, mv5)
```

```python
import functools
import math

import jax
import jax.numpy as jnp
from jax import lax
from jax.experimental import pallas as pl
from jax.experimental.pallas import tpu as pltpu

F32 = jnp.float32
BF16 = jnp.bfloat16
NT_DIMS = (((1,), (1,)), ((), ()))
TN_DIMS = (((0,), (0,)), ((), ()))

PAST_LEN = 16384
SSD_CHUNK = 128
POOL_WINDOWS = (2, 4, 8, 16)
POOL_CTX = 15
LN_EPS = 1e-5
RMS_EPS = 1e-5

V7X_LANES = 128
V7X_SUBLANES = 8
V7X_VMEM_BYTES = 64 * 1024 * 1024
VMEM_LIMIT = V7X_VMEM_BYTES - 8 * 1024 * 1024


def _tile(dim, pref, quantum):
    best = None
    t = quantum
    while t <= min(dim, pref):
        if dim % t == 0:
            best = t
        t += quantum
    return best if best is not None else dim


def _params(*sem):
    return pltpu.CompilerParams(dimension_semantics=sem, vmem_limit_bytes=VMEM_LIMIT)


def _silu(x):
    return x * jax.nn.sigmoid(x)


def _softplus(x):
    return jnp.maximum(x, 0.0) + jnp.log1p(jnp.exp(-jnp.abs(x)))


def _layer_norm(y, g, b):
    mu = jnp.mean(y, -1, keepdims=True)
    d = y - mu
    var = jnp.mean(d * d, -1, keepdims=True)
    return d * lax.rsqrt(var + LN_EPS) * g + b


def _split3(x):
    hi = x.astype(BF16)
    r = x - hi.astype(F32)
    mid = r.astype(BF16)
    lo = (r - mid.astype(F32)).astype(BF16)
    return hi, mid, lo


def _dot_sel(x, sel):
    hi, mid, lo = _split3(x)
    return (jnp.dot(hi, sel, preferred_element_type=F32) + jnp.dot(mid, sel, preferred_element_type=F32)
            + jnp.dot(lo, sel, preferred_element_type=F32))


def _sel_dot(sel, x):
    hi, mid, lo = _split3(x)
    return (jnp.dot(sel, hi, preferred_element_type=F32) + jnp.dot(sel, mid, preferred_element_type=F32)
            + jnp.dot(sel, lo, preferred_element_type=F32))


def _mm_kernel(*refs, act, transposed, n_side):
    x_ref, w_ref = refs[:2]
    o_ref = refs[2 + n_side]
    scratch = refs[3 + 2 * n_side:]
    for side_ref, side_out_ref in zip(refs[2:2 + n_side], refs[3 + n_side:3 + 2 * n_side]):
        side_out_ref[...] = side_ref[...].astype(BF16)
    if scratch:
        wb_ref, = scratch

        @pl.when(pl.program_id(1) == 0)
        def _():
            wb_ref[...] = (w_ref[0] if transposed else w_ref[...]).astype(BF16)

        w = wb_ref[...]
    else:
        w = w_ref[...]
    acc = lax.dot_general(x_ref[...], w, NT_DIMS if transposed else (((1,), (0,)), ((), ())),
                          preferred_element_type=F32)
    if act == "relu2":
        acc = jnp.square(jnp.maximum(acc, 0.0))
    o_ref[...] = acc.astype(o_ref.dtype)


def matmul(x, w, layer, out_dtype, act=None, transposed=False, col0=0, n_cols=None, sides=(), bm=1024, bn=1024):
    M, K = x.shape
    if transposed:
        row_q = V7X_SUBLANES * (2 if w.dtype == BF16 else 1)
        assert col0 % row_q == 0 and (w.dtype == BF16) == (layer is None)
        N = n_cols
    else:
        assert col0 == 0 and n_cols is None
        N = w.shape[-1]
    bm = _tile(M, bm, V7X_SUBLANES)
    bn = _tile(N, bn, V7X_LANES)
    n_i = M // bm
    if transposed and layer is None:
        w_spec = pl.BlockSpec((pl.Element(bn), pl.Element(K)),
                              lambda j, i: (pl.multiple_of(col0 + j * bn, row_q), 0))
    elif transposed:
        w_spec = pl.BlockSpec((pl.Element(1), pl.Element(bn), pl.Element(K)),
                              lambda j, i: (layer, pl.multiple_of(col0 + j * bn, row_q), 0))
    elif layer is None:
        w_spec = pl.BlockSpec((K, bn), lambda j, i: (0, j))
    else:
        w_spec = pl.BlockSpec((None, K, bn), lambda j, i: (layer, 0, j))
    in_specs = [pl.BlockSpec((bm, K), lambda j, i: (i, 0)), w_spec]
    out_specs = [pl.BlockSpec((bm, bn), lambda j, i: (i, j))]
    out_shape = [jax.ShapeDtypeStruct((M, N), out_dtype)]
    args = [x, w]
    n_steps = (N // bn) * n_i
    for s_arr, s_layer in sides:
        _, s_rows, s_cols = s_arr.shape
        rb = next((r for r in range(16, s_rows + 1, 16) if s_rows % r == 0 and s_rows // r <= n_steps), s_rows)
        last = s_rows // rb - 1
        in_specs.append(pl.BlockSpec((None, rb, s_cols), functools.partial(
            lambda l, last, j, i: (l, jnp.minimum(j * n_i + i, last), 0), s_layer, last)))
        out_specs.append(pl.BlockSpec((rb, s_cols), functools.partial(
            lambda last, j, i: (jnp.minimum(j * n_i + i, last), 0), last)))
        out_shape.append(jax.ShapeDtypeStruct((s_rows, s_cols), BF16))
        args.append(s_arr)
    scratch = [pltpu.VMEM((bn, K) if transposed else (K, bn), BF16)] if w.dtype != BF16 else []
    outs = pl.pallas_call(
        functools.partial(_mm_kernel, act=act, transposed=transposed, n_side=len(sides)),
        grid=(N // bn, n_i),
        in_specs=in_specs,
        out_specs=out_specs,
        out_shape=out_shape,
        scratch_shapes=scratch,
        compiler_params=_params("arbitrary", "arbitrary"),
    )(*args)
    return outs[0] if not sides else outs


def _kv_proj_kernel(x_ref, w_ref, o2_ref, o5_ref, wb_ref, *, heads):
    @pl.when(pl.program_id(1) == 0)
    def _():
        wb_ref[...] = w_ref[...].astype(BF16)

    acc = jnp.dot(x_ref[...], wb_ref[...], preferred_element_type=F32)
    o2_ref[...] = acc
    hd = acc.shape[1] // heads
    for h in range(heads):
        o5_ref[0, :, h, :] = acc[:, h * hd:(h + 1) * hd]


def kv_proj(mem_b, w, nb, heads):
    M, K = mem_b.shape
    depth, _, N = w.shape
    mem_len = M // nb
    hd = N // heads
    return pl.pallas_call(
        functools.partial(_kv_proj_kernel, heads=heads),
        grid=(depth, nb),
        in_specs=[pl.BlockSpec((mem_len, K), lambda l, b: (b, 0)),
                  pl.BlockSpec((None, K, N), lambda l, b: (l, 0, 0), pipeline_mode=pl.Buffered(1))],
        out_specs=[pl.BlockSpec((None, mem_len, N), lambda l, b: (l, b, 0)),
                   pl.BlockSpec((None, 1, mem_len, heads, hd), lambda l, b: (l, b, 0, 0, 0))],
        out_shape=[jax.ShapeDtypeStruct((depth, M, N), F32),
                   jax.ShapeDtypeStruct((depth, nb, mem_len, heads, hd), F32)],
        scratch_shapes=[pltpu.VMEM((K, N), BF16)],
        compiler_params=_params("parallel", "arbitrary"),
    )(mem_b, w)


def _mm_ln_kernel(*refs, alpha, n_first, two_x, two_res):
    refs = list(refs)
    x_refs = [refs.pop(0) for _ in range(2 if two_x else 1)]
    w_ref = refs.pop(0)
    res_refs = [refs.pop(0) for _ in range(2 if two_res else 1)]
    g_ref, b_ref, of_ref, ob_ref = refs

    bm = of_ref.shape[0]
    n_chunks = 2 if bm >= 512 else 1
    ch = bm // n_chunks

    def body(src_ref, res_ref):
        for c in range(n_chunks):
            rows = slice(c * ch, (c + 1) * ch)
            m = jnp.dot(src_ref[rows, :], w_ref[...], preferred_element_type=F32)
            out = _layer_norm(alpha * res_ref[rows, :] + m, g_ref[...], b_ref[...])
            of_ref[rows, :] = out
            ob_ref[rows, :] = out.astype(BF16)

    if not (two_x or two_res):
        body(x_refs[0], res_refs[0])
    else:
        i = pl.program_id(0)

        @pl.when(i < n_first)
        def _():
            body(x_refs[0], res_refs[0])

        @pl.when(i >= n_first)
        def _():
            body(x_refs[-1], res_refs[-1])


def matmul_ln(x, w, layer, res, g, b, alpha, x2=None, res2=None):
    M1, K = x.shape
    M2 = 0 if x2 is None else x2.shape[0]
    M = M1 + M2
    N = w.shape[-1]
    assert res2 is None or (x2 is not None and res.shape[0] == M1 and res2.shape[0] == M2)
    n_x = 1 if x2 is None else 2
    n_res = 1 if res2 is None else 2
    w_bytes = K * N * 2
    per_row = n_x * 2 * (K * 2) + n_res * 2 * (N * 4) + 2 * (N * 4) + 2 * (N * 2) + 2 * (N * 4)
    bm = _tile(math.gcd(M1, M2) if M2 else M1, min(512, (VMEM_LIMIT - w_bytes) // per_row), 2 * V7X_SUBLANES)
    n_first = M1 // bm

    def row_specs(width, two):
        if not two:
            return [pl.BlockSpec((bm, width), lambda i: (i, 0))]
        return [pl.BlockSpec((bm, width), lambda i: (jnp.minimum(i, n_first - 1), 0)),
                pl.BlockSpec((bm, width), lambda i: (jnp.maximum(i - n_first, 0), 0))]

    if layer is None:
        w_spec = pl.BlockSpec((K, N), lambda i: (0, 0), pipeline_mode=pl.Buffered(1))
    else:
        w_spec = pl.BlockSpec((None, K, N), lambda i: (layer, 0, 0), pipeline_mode=pl.Buffered(1))
    return pl.pallas_call(
        functools.partial(_mm_ln_kernel, alpha=alpha, n_first=n_first, two_x=x2 is not None,
                          two_res=res2 is not None),
        grid=(M // bm,),
        in_specs=row_specs(K, x2 is not None) + [w_spec] + row_specs(N, res2 is not None) + [
            pl.BlockSpec((1, N), lambda i: (0, 0)),
            pl.BlockSpec((1, N), lambda i: (0, 0))],
        out_specs=[pl.BlockSpec((bm, N), lambda i: (i, 0)),
                   pl.BlockSpec((bm, N), lambda i: (i, 0))],
        out_shape=[jax.ShapeDtypeStruct((M, N), F32), jax.ShapeDtypeStruct((M, N), BF16)],
        compiler_params=_params("arbitrary"),
    )(*([x] if x2 is None else [x, x2]), w, *([res] if res2 is None else [res, res2]), g, b)


def _prompt_mixer_kernel(z_ref, xs_ref, bc_ref, gb_ref, gc_ref, hh_ref, dt_ref, convw_ref, convb_ref, dtb_ref,
                         alog_ref, dskip_ref, normg_ref, scw_ref, tril_ref, expand_ref,
                         y_ref, hout_ref, utail_ref, ctail_ref, state_ref, ext_ref, act_ref, extu_ref,
                         *, L, nc, G, R, P, N):
    RP = R * P
    D = G * RP
    GN = G * N
    CH = D + 2 * GN
    KC = convw_ref.shape[0]
    KS = scw_ref.shape[0]
    c = pl.program_id(1)

    @pl.when(c == 0)
    def _init():
        ext_ref[0:8, :] = jnp.zeros((8, CH), F32)
        extu_ref[0:8, :] = jnp.zeros((8, D), F32)
        state_ref[...] = jnp.zeros(state_ref.shape, F32)

    ext_ref[8:8 + L, 0:D] = xs_ref[...]
    ext_ref[8:8 + L, D:CH] = bc_ref[...]
    cw = min(512, CH)
    for j in range(CH // cw):
        cols = slice(j * cw, (j + 1) * cw)
        base = 8 - (KC - 1)
        acc = ext_ref[base:base + L, cols] * convw_ref[0:1, cols]
        for k in range(1, KC):
            acc = acc + ext_ref[base + k:base + k + L, cols] * convw_ref[k:k + 1, cols]
        act_ref[:, cols] = _silu(acc + convb_ref[:, cols])

    dt = _softplus(dt_ref[...] + dtb_ref[...])
    a = dt * (-jnp.exp(alog_ref[...]))
    acs = _sel_dot(tril_ref[...], a)
    dt_e = _dot_sel(dt, expand_ref[...])
    acs_e = _dot_sel(acs, expand_ref[...])
    acs_row = acs.T

    li = lax.broadcasted_iota(jnp.int32, (L, L), 0)
    si = lax.broadcasted_iota(jnp.int32, (L, L), 1)
    causal = si <= li
    ci = lax.broadcasted_iota(jnp.int32, (L, RP), 1)
    head_cols = [jnp.logical_and(ci >= r * P, ci < (r + 1) * P) for r in range(R)]

    for g in range(G):
        gc_ = slice(g * RP, (g + 1) * RP)
        xs_g = act_ref[:, gc_]
        b_g = act_ref[:, D + g * N:D + (g + 1) * N]
        c_g = act_ref[:, D + GN + g * N:D + GN + (g + 1) * N].astype(BF16)
        xdt = xs_g * dt_e[:, gc_]
        x_bd = jnp.concatenate([jnp.where(head_cols[r], xdt, 0.0) for r in range(R)], axis=0).astype(BF16)
        acs_g = acs_e[:, gc_]
        bt_g = b_g.T
        cb = jnp.dot(c_g, bt_g.astype(BF16), preferred_element_type=F32)

        ms, dbs = [], []
        for r in range(R):
            h = g * R + r
            col = acs[:, h:h + 1]
            row = acs_row[h:h + 1, :]
            ms.append(cb * jnp.exp(jnp.where(causal, col - row, -jnp.inf)))
            dbs.append(bt_g * jnp.exp(acs_row[h:h + 1, L - 1:L] - row))

        y = jnp.dot(jnp.concatenate(ms, axis=1).astype(BF16), x_bd, preferred_element_type=F32)
        h_prev = state_ref[g]
        y_off = jnp.dot(c_g, h_prev.astype(BF16), preferred_element_type=F32)
        dbt = jnp.concatenate(dbs, axis=1).astype(BF16)
        new_states = jnp.dot(dbt, x_bd, preferred_element_type=F32)
        state_ref[g] = h_prev * jnp.exp(acs_g[L - 1:L, :]) + new_states
        y = y + y_off * jnp.exp(acs_g)
        y = y + dskip_ref[:, gc_] * xs_g
        y = y * _silu(z_ref[:, gc_])
        y = y * lax.rsqrt(jnp.mean(y * y, -1, keepdims=True) + RMS_EPS)
        y_ref[:, gc_] = (y * normg_ref[:, gc_]).astype(BF16)

    cw = min(512, D)
    for j in range(D // cw):
        cols = slice(j * cw, (j + 1) * cw)
        extu_ref[8:8 + L, cols] = gc_ref[:, cols] * hh_ref[:, cols]
        base = 8 - (KS - 1)
        v = extu_ref[base:base + L, cols] * scw_ref[0:1, cols]
        for k in range(1, KS):
            v = v + extu_ref[base + k:base + k + L, cols] * scw_ref[k:k + 1, cols]
        y_ref[:, D + j * cw:D + (j + 1) * cw] = (gb_ref[:, cols] * v).astype(BF16)

    @pl.when(c == nc - 1)
    def _fin():
        utail_ref[0] = extu_ref[L:L + 8, :]
        ctail_ref[0] = ext_ref[L:L + 8, :]
        for g in range(G):
            hout_ref[0, g * R:(g + 1) * R] = state_ref[g].T.reshape(R, P, N)

    ext_ref[0:8, :] = ext_ref[L:L + 8, :]
    extu_ref[0:8, :] = extu_ref[L:L + 8, :]


def _mixer_consts(D, P):
    expand = (jnp.arange(V7X_LANES)[:, None] == (jnp.arange(D) // P)[None, :]).astype(BF16)
    return expand


def _proj_specs(rows, D, row_block):
    return [pl.BlockSpec((rows, D), functools.partial(lambda j, *idx: (row_block(*idx), j), j)) for j in (0, 1, 2)] * 2


def prompt_mixer(proj_a, proj_b, dt_raw, nb, T, cfg, wts):
    D, G, R, P, N = cfg["D"], cfg["G"], cfg["R"], cfg["P"], cfg["N"]
    H = G * R
    GN = G * N
    CH = D + 2 * GN
    L = SSD_CHUNK
    nc = T // L
    assert T % L == 0 and L == V7X_LANES and 2 * GN == D and R * P * G == D
    tril = jnp.tril(jnp.ones((L, L), BF16))
    expand = _mixer_consts(D, P)
    consts = list(wts) + [tril, expand]

    def row_block(b, c):
        return b * nc + c

    in_specs = (_proj_specs(L, D, row_block) + [pl.BlockSpec((L, V7X_LANES), lambda b, c: (b * nc + c, 0))]
                + [pl.BlockSpec(a.shape, functools.partial(lambda n, b, c: (0,) * n, a.ndim)) for a in consts])
    return pl.pallas_call(
        functools.partial(_prompt_mixer_kernel, L=L, nc=nc, G=G, R=R, P=P, N=N),
        grid=(nb, nc),
        in_specs=in_specs,
        out_specs=[pl.BlockSpec((L, 2 * D), lambda b, c: (b * nc + c, 0)),
                   pl.BlockSpec((1, H, P, N), lambda b, c: (b, 0, 0, 0)),
                   pl.BlockSpec((1, 8, D), lambda b, c: (b, 0, 0)),
                   pl.BlockSpec((1, 8, CH), lambda b, c: (b, 0, 0))],
        out_shape=[jax.ShapeDtypeStruct((nb * T, 2 * D), BF16),
                   jax.ShapeDtypeStruct((nb, H, P, N), F32),
                   jax.ShapeDtypeStruct((nb, 8, D), F32),
                   jax.ShapeDtypeStruct((nb, 8, CH), F32)],
        scratch_shapes=[pltpu.VMEM((G, N, R * P), F32), pltpu.VMEM((L + 8, CH), F32),
                        pltpu.VMEM((L, CH), F32), pltpu.VMEM((L + 8, D), F32)],
        compiler_params=_params("parallel", "arbitrary"),
    )(*([proj_a] * 3 + [proj_b] * 3 + [dt_raw] + consts))


def _sample_mixer_kernel(*refs, bb, T, G, R, P, N, n_prev):
    (z_ref, xs_ref, bc_ref, gb_ref, gc_ref, hh_ref, dt_ref, convw_ref, convb_ref, dtb_ref, alog_ref,
     dskip_ref, normg_ref, scw_ref, btril_ref, expand_ref, expand_t_ref, rep_ref,
     h0_ref, ctx_ref, scctx_ref) = refs[:21]
    prev_refs = refs[21:21 + n_prev]
    y_ref, hout_ref, utail_ref, ctail_ref, ext_ref, act_ref, extu_ref = refs[21 + n_prev:]
    H = G * R
    RP = R * P
    D = G * RP
    GN = G * N
    CH = D + 2 * GN
    HT = H * T
    KC = convw_ref.shape[0]
    KS = scw_ref.shape[0]
    rows = bb * T

    ext_ref[:, 8 - (KC - 1):8, :] = ctx_ref[...]
    ext_ref[:, 8:8 + T, 0:D] = xs_ref[...].reshape(bb, T, D)
    ext_ref[:, 8:8 + T, D:CH] = bc_ref[...].reshape(bb, T, 2 * GN)
    ctail_ref[...] = ext_ref[:, T:T + 8, :]
    cw = min(512, CH)
    for j in range(CH // cw):
        cols = slice(j * cw, (j + 1) * cw)
        base = 8 - (KC - 1)
        acc = ext_ref[:, base:base + T, cols] * convw_ref[0:1, cols]
        for k in range(1, KC):
            acc = acc + ext_ref[:, base + k:base + k + T, cols] * convw_ref[k:k + 1, cols]
        act_ref[:, cols] = _silu(acc + convb_ref[:, cols]).reshape(rows, cw)

    dt = _softplus(dt_ref[...] + dtb_ref[...])
    a = dt * (-jnp.exp(alog_ref[...]))
    acs = _sel_dot(btril_ref[...], a)
    dt_e = _dot_sel(dt, expand_ref[...])
    acs_e = _dot_sel(acs, expand_ref[...])
    acs_e3 = acs_e.reshape(bb, T, D)
    last_e = acs_e3[:, T - 1:T, :]
    xs_act = act_ref[:, 0:D]
    xdt = xs_act * dt_e
    xde = xdt * jnp.exp(last_e - acs_e3).reshape(rows, D)
    a1 = _dot_sel(acs, expand_t_ref[...]).reshape(bb, T, HT)
    l3 = lax.broadcasted_iota(jnp.int32, (bb, T, HT), 1)
    s3 = lax.broadcasted_iota(jnp.int32, (bb, T, HT), 2) & (T - 1)
    a2 = jnp.sum(jnp.where(l3 == s3, a1, 0.0), axis=1, keepdims=True)
    decay = jnp.exp(jnp.where(s3 <= l3, a1 - a2, -jnp.inf)).reshape(rows, HT)
    cd_heads = jnp.exp(acs)

    ri = lax.broadcasted_iota(jnp.int32, (HT, D), 0)
    ci = lax.broadcasted_iota(jnp.int32, (HT, D), 1)
    bd_mask = (ri >> (T.bit_length() - 1)) == (ci >> (P.bit_length() - 1))
    lane_g = lax.broadcasted_iota(jnp.int32, (T, HT), 1) >> ((R * T).bit_length() - 1)

    y_diag, y_off = [], []
    for b in range(bb):
        rs = slice(b * T, (b + 1) * T)
        b_pieces = [act_ref[rs, D + g * N:D + (g + 1) * N] for g in range(G)]
        c_st = jnp.concatenate([act_ref[rs, D + GN + g * N:D + GN + (g + 1) * N] for g in range(G)],
                               axis=0).astype(BF16)
        b_st = jnp.concatenate(b_pieces, axis=0).astype(BF16)
        call = lax.dot_general(c_st, b_st, NT_DIMS, preferred_element_type=F32)
        q = _dot_sel(call, rep_ref[...])
        cb = jnp.zeros((T, HT), F32)
        for g in range(G):
            cb = cb + jnp.where(lane_g == g, q[g * T:(g + 1) * T], 0.0)
        mcat = (cb * decay[rs]).astype(BF16)
        x_bd = jnp.where(bd_mask, jnp.concatenate([xdt[rs]] * H, axis=0), 0.0).astype(BF16)
        y_diag.append(jnp.dot(mcat, x_bd, preferred_element_type=F32))

        h0_b = h0_ref[b].reshape(D, N)
        yy = lax.dot_general(c_st, h0_b.astype(BF16), NT_DIMS, preferred_element_type=F32)
        y_off.append(jnp.concatenate([yy[g * T:(g + 1) * T, g * RP:(g + 1) * RP] for g in range(G)], axis=1))

        x_bd2 = jnp.where(bd_mask, jnp.concatenate([xde[rs]] * H, axis=0), 0.0).astype(BF16)
        b_rep = jnp.concatenate([b_pieces[h // R] for h in range(H)], axis=0).astype(BF16)
        new_states = lax.dot_general(x_bd2, b_rep, TN_DIMS, preferred_element_type=F32)
        r_last = b * T + T - 1
        cd = jnp.concatenate([jnp.broadcast_to(cd_heads[r_last:r_last + 1, h:h + 1], (P, N)) for h in range(H)],
                             axis=0)
        hout_ref[n_prev, b] = (h0_b * cd + new_states).reshape(H, P, N)
    for s, prev_ref in enumerate(prev_refs):
        hout_ref[s] = prev_ref[...]

    y = jnp.concatenate(y_diag, axis=0) + jnp.concatenate(y_off, axis=0) * jnp.exp(acs_e)
    y = y + dskip_ref[...] * xs_act
    y = y * _silu(z_ref[...])
    for g in range(G):
        gc_ = slice(g * RP, (g + 1) * RP)
        yg = y[:, gc_]
        yg = yg * lax.rsqrt(jnp.mean(yg * yg, -1, keepdims=True) + RMS_EPS)
        y_ref[:, gc_] = (yg * normg_ref[:, gc_]).astype(BF16)

    extu_ref[:, 8 - (KS - 1):8, :] = scctx_ref[...]
    cw = min(512, D)
    for j in range(D // cw):
        cols = slice(j * cw, (j + 1) * cw)
        extu_ref[:, 8:8 + T, cols] = (gc_ref[:, cols] * hh_ref[:, cols]).reshape(bb, T, cw)
        base = 8 - (KS - 1)
        v = extu_ref[:, base:base + T, cols] * scw_ref[0:1, cols]
        for k in range(1, KS):
            v = v + extu_ref[:, base + k:base + k + T, cols] * scw_ref[k:k + 1, cols]
        y_ref[:, D + j * cw:D + (j + 1) * cw] = (gb_ref[:, cols] * v.reshape(rows, cw)).astype(BF16)
    utail_ref[...] = extu_ref[:, T:T + 8, :]


def sample_mixer(proj_a, proj_b, dt_raw, row0, nb, T, cfg, wts, state_ssd, state_ssd_conv, state_short_conv,
                 layer_e, prev_states=(), bb=4):
    D, G, R, P, N = cfg["D"], cfg["G"], cfg["R"], cfg["P"], cfg["N"]
    H = G * R
    GN = G * N
    CH = D + 2 * GN
    bb = _tile(nb, bb, 2)
    rows = bb * T
    pow2 = lambda v: v & (v - 1) == 0
    assert T == V7X_SUBLANES and row0 % rows == 0 and 2 * GN == D and R * P * G == D
    assert pow2(P) and pow2(R)
    rb0 = row0 // rows
    kc1, ks1 = state_ssd_conv.shape[2], state_short_conv.shape[2]
    assert kc1 <= T and ks1 <= T
    seq = jnp.arange(rows) // T
    btril = jnp.logical_and(seq[:, None] == seq[None, :], jnp.arange(rows)[:, None] >= jnp.arange(rows)[None, :])
    expand = _mixer_consts(D, P)
    lane_h = jnp.arange(H * T) // T
    expand_t = (jnp.arange(V7X_LANES)[:, None] == lane_h[None, :]).astype(BF16)
    rep = jnp.logical_and((jnp.arange(G * T) // T)[:, None] == (lane_h // R)[None, :],
                          (jnp.arange(G * T) % T)[:, None] == (jnp.arange(H * T) % T)[None, :]).astype(BF16)
    consts = list(wts) + [btril.astype(BF16), expand, expand_t, rep]
    n_prev = len(prev_states)

    in_specs = (_proj_specs(rows, D, lambda i: rb0 + i) + [pl.BlockSpec((rows, V7X_LANES), lambda i: (rb0 + i, 0))]
                + [pl.BlockSpec(a.shape, functools.partial(lambda n, i: (0,) * n, a.ndim)) for a in consts]
                + [pl.BlockSpec((None, bb, H, P, N), lambda i: (layer_e, i, 0, 0, 0)),
                   pl.BlockSpec((None, bb, kc1, CH), lambda i: (layer_e, i, 0, 0)),
                   pl.BlockSpec((None, bb, ks1, D), lambda i: (layer_e, i, 0, 0))]
                + [pl.BlockSpec((bb, H, P, N), lambda i: (i, 0, 0, 0))] * n_prev)
    return pl.pallas_call(
        functools.partial(_sample_mixer_kernel, bb=bb, T=T, G=G, R=R, P=P, N=N, n_prev=n_prev),
        grid=(nb // bb,),
        in_specs=in_specs,
        out_specs=[pl.BlockSpec((rows, 2 * D), lambda i: (i, 0)),
                   pl.BlockSpec((n_prev + 1, bb, H, P, N), lambda i: (0, i, 0, 0, 0)),
                   pl.BlockSpec((bb, 8, D), lambda i: (i, 0, 0)),
                   pl.BlockSpec((bb, 8, CH), lambda i: (i, 0, 0))],
        out_shape=[jax.ShapeDtypeStruct((nb * T, 2 * D), BF16),
                   jax.ShapeDtypeStruct((n_prev + 1, nb, H, P, N), F32),
                   jax.ShapeDtypeStruct((nb, 8, D), F32),
                   jax.ShapeDtypeStruct((nb, 8, CH), F32)],
        scratch_shapes=[pltpu.VMEM((bb, 8 + T, CH), F32), pltpu.VMEM((rows, CH), F32),
                        pltpu.VMEM((bb, 8 + T, D), F32)],
        compiler_params=_params("parallel"),
    )(*([proj_a] * 3 + [proj_b] * 3 + [dt_raw] + consts + [state_ssd, state_ssd_conv, state_short_conv]
        + list(prev_states)))


def _pool_sample_kernel(st_ref, x_ref, p_ref, ext_ref, *, start_pos, dg):
    T = x_ref.shape[1]
    ext_ref[:, 16 - POOL_CTX:16, :] = st_ref[...]
    ext_ref[:, 16:16 + T, :] = x_ref[...]
    pos = start_pos + lax.broadcasted_iota(jnp.int32, (1, T, 1), 1)
    for gi, w in enumerate(POOL_WINDOWS):
        cols = slice(gi * dg, (gi + 1) * dg)
        s = ext_ref[:, 16:16 + T, cols]
        for j in range(1, w):
            s = s + ext_ref[:, 16 - j:16 - j + T, cols]
        cnt = jnp.minimum(pos + 1, w).astype(F32)
        p_ref[:, :, cols] = (s / cnt - x_ref[:, :, cols]).astype(BF16)


def pool_sample(state_pool, layer_o, x3, dg):
    B, T, D = x3.shape
    bb = _tile(B, 16, 1)
    return pl.pallas_call(
        functools.partial(_pool_sample_kernel, start_pos=PAST_LEN, dg=dg),
        grid=(B // bb,),
        in_specs=[pl.BlockSpec((None, bb, POOL_CTX, D), lambda i: (layer_o, i, 0, 0)),
                  pl.BlockSpec((bb, T, D), lambda i: (i, 0, 0))],
        out_specs=pl.BlockSpec((bb, T, D), lambda i: (i, 0, 0)),
        out_shape=jax.ShapeDtypeStruct((B, T, D), BF16),
        scratch_shapes=[pltpu.VMEM((bb, 16 + T, D), F32)],
        compiler_params=_params("parallel"),
    )(state_pool, x3)


POOL_TOP = 8 + 16


def _pool_ln_kernel(x_ref, halo_ref, ps_ref, w_ref, scale_ref, g_ref, b_ref, of_ref, ob_ref, ext_ref, tmp_ref, p_ref,
                    *, n_prompt_tiles, tiles_per_seq, dg, alpha):
    i = pl.program_id(0)
    bm = x_ref.shape[0]
    top = POOL_TOP
    end = top + bm

    @pl.when(i < n_prompt_tiles)
    def _prompt():
        first = (i % tiles_per_seq) == 0
        ext_ref[0:8, :] = jnp.zeros((8, ext_ref.shape[1]), F32)
        ext_ref[8:top, :] = jnp.where(first, 0.0, halo_ref[...])
        ext_ref[top:end, :] = x_ref[...]
        pos = (i % tiles_per_seq) * bm + lax.broadcasted_iota(jnp.int32, (bm, 1), 0)
        for gi, w in enumerate(POOL_WINDOWS):
            cols = slice(gi * dg, (gi + 1) * dg)
            src, src_cols, k, level = ext_ref, cols, 1, 0
            while 2 * k < w:
                dst = tmp_ref.at[level % 2]
                dst[0:8, :] = jnp.zeros((8, dg), F32)
                dst[8:end, :] = src[8:end, src_cols] + src[8 - k:end - k, src_cols]
                src, src_cols, k, level = dst, slice(None), 2 * k, level + 1
            s = src[top:end, src_cols] + src[top - k:end - k, src_cols]
            cnt = jnp.minimum(pos + 1, w).astype(F32)
            p_ref[:, cols] = (s / cnt - x_ref[:, cols]).astype(BF16)

    @pl.when(i >= n_prompt_tiles)
    def _sample():
        p_ref[...] = ps_ref[...]

    for gi in range(len(POOL_WINDOWS)):
        cols = slice(gi * dg, (gi + 1) * dg)
        m = jnp.dot(p_ref[:, cols], w_ref[gi], preferred_element_type=F32) * scale_ref[:, cols]
        ext_ref[top:end, cols] = alpha * x_ref[:, cols] + m
    out = _layer_norm(ext_ref[top:end, :], g_ref[...], b_ref[...])
    of_ref[...] = out
    ob_ref[...] = out.astype(BF16)


def pool_ln(x, p_sample, w_pool, scale, g, b, n_prompt_rows, T, alpha, bm=512):
    M, D = x.shape
    dg = D // len(POOL_WINDOWS)
    bm = _tile(math.gcd(math.gcd(n_prompt_rows, M - n_prompt_rows), T), bm, 16)
    npt = n_prompt_rows // bm
    hb = bm // 16
    return pl.pallas_call(
        functools.partial(_pool_ln_kernel, n_prompt_tiles=npt, tiles_per_seq=T // bm, dg=dg, alpha=alpha),
        grid=(M // bm,),
        in_specs=[pl.BlockSpec((bm, D), lambda i: (i, 0)),
                  pl.BlockSpec((16, D), lambda i: (jnp.maximum(i * hb - 1, 0), 0)),
                  pl.BlockSpec((bm, D), lambda i: (jnp.maximum(i - npt, 0), 0)),
                  pl.BlockSpec(w_pool.shape, lambda i: (0, 0, 0)),
                  pl.BlockSpec((1, D), lambda i: (0, 0)),
                  pl.BlockSpec((1, D), lambda i: (0, 0)),
                  pl.BlockSpec((1, D), lambda i: (0, 0))],
        out_specs=[pl.BlockSpec((bm, D), lambda i: (i, 0)), pl.BlockSpec((bm, D), lambda i: (i, 0))],
        out_shape=[jax.ShapeDtypeStruct((M, D), F32), jax.ShapeDtypeStruct((M, D), BF16)],
        scratch_shapes=[pltpu.VMEM((POOL_TOP + bm, D), F32), pltpu.VMEM((2, POOL_TOP + bm, dg), F32),
                        pltpu.VMEM((bm, D), BF16)],
        compiler_params=_params("arbitrary"),
    )(x, x, p_sample, w_pool, scale, g, b)


def _softmax_rows(s):
    m = jnp.max(s, -1, keepdims=True)
    e = jnp.exp(s - m)
    return e / jnp.sum(e, -1, keepdims=True)


def _attn_prompt_kernel(q_ref, k_ref, v_ref, o_ref, *, scale):
    s = lax.dot_general(q_ref[...], k_ref[...].astype(BF16), (((1,), (1,)), ((), ())),
                        preferred_element_type=F32) * scale
    p = _softmax_rows(s).astype(BF16)
    o_ref[...] = jnp.dot(p, v_ref[...].astype(BF16), preferred_element_type=F32).astype(BF16)


def attn_prompt(q, mk, mv, layer, nb, T, mem_len, heads, tq=2048):
    M, D = q.shape
    hd = D // heads
    tq = _tile(T, tq, V7X_SUBLANES)
    nt = T // tq
    return pl.pallas_call(
        functools.partial(_attn_prompt_kernel, scale=hd ** -0.5),
        grid=(nb, heads, nt),
        in_specs=[pl.BlockSpec((tq, hd), lambda b, h, t: (b * nt + t, h)),
                  pl.BlockSpec((None, mem_len, hd), lambda b, h, t: (layer, b, h)),
                  pl.BlockSpec((None, mem_len, hd), lambda b, h, t: (layer, b, h))],
        out_specs=pl.BlockSpec((tq, hd), lambda b, h, t: (b * nt + t, h)),
        out_shape=jax.ShapeDtypeStruct((M, D), BF16),
        input_output_aliases={0: 0},
        compiler_params=_params("parallel", "parallel", "parallel"),
    )(q, mk, mv)


def _attn_sample_kernel(q_ref, k_hbm, v_hbm, o_ref, kbuf, vbuf, sem, *, layer, scale, T):
    _, heads, bb, mem_len, hd = kbuf.shape
    i = pl.program_id(0)
    n = pl.num_programs(0)

    def copies(step, slot):
        cps = []
        for h in range(heads):
            for t, (src, dst) in enumerate(((k_hbm, kbuf), (v_hbm, vbuf))):
                cps.append(pltpu.make_async_copy(src.at[layer, pl.ds(step * bb, bb), :, h, :],
                                                 dst.at[slot, h], sem.at[slot, t, h]))
        return cps

    @pl.when(i == 0)
    def _():
        for cp in copies(0, 0):
            cp.start()

    slot = i % 2

    @pl.when(i + 1 < n)
    def _():
        for cp in copies(i + 1, 1 - slot):
            cp.start()

    for cp in copies(i, slot):
        cp.wait()

    slots = bb * heads
    q_all = q_ref[...].astype(F32)
    st = jnp.zeros((mem_len, slots * T), F32)
    for b in range(bb):
        for h in range(heads):
            s_ = b * heads + h
            q_bh = q_all[b * T:(b + 1) * T, h * hd:(h + 1) * hd]
            parts = []
            if s_:
                parts.append(jnp.zeros((s_ * T, hd), F32))
            parts.append(q_bh)
            if s_ + 1 < slots:
                parts.append(jnp.zeros(((slots - s_ - 1) * T, hd), F32))
            q_slot = jnp.concatenate(parts, axis=0).astype(BF16)
            st = st + lax.dot_general(kbuf[slot, h, b].astype(BF16), q_slot, (((1,), (1,)), ((), ())),
                                      preferred_element_type=F32)
    st = st * scale
    e = jnp.exp(st - jnp.max(st, 0, keepdims=True))
    p_all = (e / jnp.sum(e, 0, keepdims=True)).T
    rows = []
    for b in range(bb):
        outs = []
        for h in range(heads):
            s_ = b * heads + h
            p = p_all[s_ * T:(s_ + 1) * T, :].astype(BF16)
            outs.append(jnp.dot(p, vbuf[slot, h, b].astype(BF16), preferred_element_type=F32))
        rows.append(jnp.concatenate(outs, axis=1))
    o_ref[...] = jnp.concatenate(rows, axis=0).astype(BF16)


def attn_sample(q, cache_k, cache_v, layer, row0, T):
    _, B, mem_len, heads, hd = cache_k.shape
    M, D = q.shape
    bb = V7X_LANES // (heads * T)
    assert bb * heads * T == V7X_LANES and B % bb == 0
    rows = bb * T
    assert row0 % rows == 0 and rows % 16 == 0
    rb0 = row0 // rows
    return pl.pallas_call(
        functools.partial(_attn_sample_kernel, layer=layer, scale=hd ** -0.5, T=T),
        grid=(B // bb,),
        in_specs=[pl.BlockSpec((rows, D), lambda i: (rb0 + i, 0)),
                  pl.BlockSpec(memory_space=pl.ANY),
                  pl.BlockSpec(memory_space=pl.ANY)],
        out_specs=pl.BlockSpec((rows, D), lambda i: (rb0 + i, 0)),
        out_shape=jax.ShapeDtypeStruct((M, D), BF16),
        input_output_aliases={0: 0},
        scratch_shapes=[pltpu.VMEM((2, heads, bb, mem_len, hd), F32),
                        pltpu.VMEM((2, heads, bb, mem_len, hd), F32),
                        pltpu.SemaphoreType.DMA((2, 2, heads))],
        compiler_params=_params("arbitrary"),
    )(q, cache_k, cache_v)


def kernel(x_prompt, x_sample, state_ssd, state_ssd_conv, state_short_conv, state_pool, cache_mem_k, cache_mem_v, mem_prompt, w_in_even, ssd_conv_w, ssd_conv_b, ssd_dt_bias, ssd_a_log, ssd_d, ssd_norm_g, sc_conv_w, w_out_even, w_pool, pool_scale, wq_x, wk_x, wv_x, wo_x, w_up, w_down, ln_g, ln_b):
    BP, T, D = x_prompt.shape
    BS, TS, _ = x_sample.shape
    depth = wq_x.shape[0]
    n_even = w_in_even.shape[0]
    _, _, H, P, N = state_ssd.shape
    CH = state_ssd_conv.shape[-1]
    G = (CH - D) // (2 * N)
    R = H // G
    cfg = dict(D=D, G=G, R=R, P=P, N=N)
    mem_len, heads, hd = cache_mem_k.shape[2:]
    dg = w_pool.shape[-1]
    alpha = (2.0 * depth) ** 0.25
    MP, MS = BP * T, BS * TS
    o_dt = D + CH
    assert ssd_d.shape[-1] == H and H <= V7X_LANES and heads * hd == D

    xp2, xs2 = x_prompt.reshape(MP, D), x_sample.reshape(MS, D)
    xb = jnp.concatenate([xp2.astype(BF16), xs2.astype(BF16)], axis=0)
    xf = None
    mem_b = mem_prompt.reshape(BP * mem_len, D).astype(BF16)
    mk2, mk5 = kv_proj(mem_b, wk_x, BP, heads)
    mv2, mv5 = kv_proj(mem_b, wv_x, BP, heads)

    w_in_t = jnp.swapaxes(w_in_even, 1, 2)
    w_dt = jnp.pad(w_in_even[:, :, o_dt:o_dt + H], ((0, 0), (0, 0), (0, V7X_LANES - H))).astype(BF16)
    w_pool_b = w_pool.astype(BF16)

    def pad_lanes(v):
        return jnp.pad(v.astype(F32), (0, V7X_LANES - v.shape[0]))[None, :]

    def last_rows(x2, nb, t, n):
        return jnp.stack([x2[(b + 1) * t - n:(b + 1) * t] for b in range(nb)])

    kc, ks = ssd_conv_w.shape[1], sc_conv_w.shape[1]
    hs_p, hs_s, cbs_p, cbs_s, sbs_p, sbs_s, pbs_p, pbs_s = [], [], [], [], [], [], [], []
    h_s = None
    for layer in range(depth):
        ln = lambda j: (ln_g[layer, j][None, :], ln_b[layer, j][None, :])
        if layer % 2 == 0:
            e = layer // 2
            if layer == 0:
                proj_a, w_out_b = matmul(xb, w_in_t, e, F32, transposed=True, col0=0, n_cols=o_dt,
                                         sides=[(w_out_even, e)])
                proj_b, w_up_b, wq_b = matmul(xb, w_in_t, e, F32, transposed=True, col0=o_dt + H, n_cols=3 * D,
                                              sides=[(w_up, 0), (wq_x, 0)])
            else:
                proj_a, w_out_b = matmul(xb, w_in_b, None, F32, transposed=True, col0=0, n_cols=o_dt,
                                         sides=[(w_out_even, e)])
                proj_b = matmul(xb, w_in_b, None, F32, transposed=True, col0=o_dt + H, n_cols=3 * D)
            dt_raw = matmul(xb, w_dt, e, F32)
            wts = (ssd_conv_w[e], ssd_conv_b[e][None, :], pad_lanes(ssd_dt_bias[e]), pad_lanes(ssd_a_log[e]),
                   jnp.repeat(ssd_d[e].astype(F32), P)[None, :], ssd_norm_g[e][None, :], sc_conv_w[e])
            y_p, h_p, ut_p, ct_p = prompt_mixer(proj_a, proj_b, dt_raw, BP, T, cfg, wts)
            y_s, h_s, ut_s, ct_s = sample_mixer(proj_a, proj_b, dt_raw, MP, BS, TS, cfg, wts,
                                                state_ssd, state_ssd_conv, state_short_conv, e,
                                                prev_states=hs_s if e == n_even - 1 else ())
            hs_s.append(h_s[0])
            hs_p.append(h_p)
            cbs_p.append(ct_p[:, 8 - (kc - 1):])
            cbs_s.append(ct_s[:, 8 - (kc - 1):])
            sbs_p.append(ut_p[:, 8 - (ks - 1):])
            sbs_s.append(ut_s[:, 8 - (ks - 1):])
            if xf is None:
                xf, xb = matmul_ln(y_p, w_out_b, None, xp2, *ln(0), alpha, x2=y_s, res2=xs2)
            else:
                xf, xb = matmul_ln(y_p, w_out_b, None, xf, *ln(0), alpha, x2=y_s)
        else:
            o = layer // 2
            xs3 = xf[MP:].reshape(BS, TS, D)
            pbs_p.append(last_rows(xf, BP, T, POOL_CTX))
            pbs_s.append(jnp.concatenate([state_pool[o][:, TS:], xs3], axis=1))
            p_s = pool_sample(state_pool, o, xs3, dg).reshape(MS, D)
            xf, xb = pool_ln(xf, p_s, w_pool_b[o], pool_scale[o][None, :], *ln(0), MP, T, alpha)

        q, wo_b = matmul(xb, wq_b, None, BF16, sides=[(wo_x, layer)])
        att = attn_prompt(q, mk2, mv2, layer, BP, T, mem_len, heads)
        att = attn_sample(att, cache_mem_k, cache_mem_v, layer, MP, TS)
        xf, xb = matmul_ln(att, wo_b, None, xf, *ln(1), alpha)

        nxt = layer + 1
        sides = [(w_down, layer)]
        if nxt < depth:
            sides += [(w_up, nxt), (wq_x, nxt)] + ([(w_in_t, nxt // 2)] if nxt % 2 == 0 else [])
        outs = matmul(xb, w_up_b, None, BF16, act="relu2", sides=sides)
        hid, w_down_b = outs[0], outs[1]
        if nxt < depth:
            w_up_b, wq_b = outs[2], outs[3]
            w_in_b = outs[4] if nxt % 2 == 0 else None
        xf, xb = matmul_ln(hid, w_down_b, None, xf, *ln(2), alpha)

    return (xf[:MP].reshape(BP, T, D), xf[MP:].reshape(BS, TS, D),
            jnp.stack(hs_p), h_s, jnp.stack(cbs_p), jnp.stack(cbs_s),
            jnp.stack(sbs_p), jnp.stack(sbs_s), jnp.stack(pbs_p), jnp.stack(pbs_s), mk5, mv5)
```

```python
import functools
import math

import jax
import jax.numpy as jnp
from jax import lax
from jax.experimental import pallas as pl
from jax.experimental.pallas import tpu as pltpu

F32 = jnp.float32
BF16 = jnp.bfloat16
NT_DIMS = (((1,), (1,)), ((), ()))
TN_DIMS = (((0,), (0,)), ((), ()))

PAST_LEN = 16384
SSD_CHUNK = 128
POOL_WINDOWS = (2, 4, 8, 16)
POOL_CTX = 15
LN_EPS = 1e-5
RMS_EPS = 1e-5

V7X_LANES = 128
V7X_SUBLANES = 8
V7X_VMEM_BYTES = 64 * 1024 * 1024
VMEM_LIMIT = V7X_VMEM_BYTES - 4 * 1024 * 1024


def _tile(dim, pref, quantum):
    best = None
    t = quantum
    while t <= min(dim, pref):
        if dim % t == 0:
            best = t
        t += quantum
    return best if best is not None else dim


def _params(*sem):
    return pltpu.CompilerParams(dimension_semantics=sem, vmem_limit_bytes=VMEM_LIMIT)


def _silu(x):
    return x * jax.nn.sigmoid(x)


def _softplus(x):
    return jnp.maximum(x, 0.0) + jnp.log1p(jnp.exp(-jnp.abs(x)))


def _layer_norm(y, g, b):
    mu = jnp.mean(y, -1, keepdims=True)
    d = y - mu
    var = jnp.mean(d * d, -1, keepdims=True)
    return d * lax.rsqrt(var + LN_EPS) * g + b


def _split3(x):
    hi = x.astype(BF16)
    r = x - hi.astype(F32)
    mid = r.astype(BF16)
    lo = (r - mid.astype(F32)).astype(BF16)
    return hi, mid, lo


def _dot_sel(x, sel):
    hi, mid, lo = _split3(x)
    return (jnp.dot(hi, sel, preferred_element_type=F32) + jnp.dot(mid, sel, preferred_element_type=F32)
            + jnp.dot(lo, sel, preferred_element_type=F32))


def _sel_dot(sel, x):
    hi, mid, lo = _split3(x)
    return (jnp.dot(sel, hi, preferred_element_type=F32) + jnp.dot(sel, mid, preferred_element_type=F32)
            + jnp.dot(sel, lo, preferred_element_type=F32))


def _mm_kernel(*refs, act, transposed, n_side):
    x_ref, w_ref = refs[:2]
    o_ref = refs[2 + n_side]
    scratch = refs[3 + 2 * n_side:]
    for side_ref, side_out_ref in zip(refs[2:2 + n_side], refs[3 + n_side:3 + 2 * n_side]):
        side_out_ref[...] = side_ref[...].astype(BF16)
    if scratch:
        wb_ref, = scratch

        @pl.when(pl.program_id(1) == 0)
        def _():
            wb_ref[...] = (w_ref[0] if transposed else w_ref[...]).astype(BF16)

        w = wb_ref[...]
    else:
        w = w_ref[...]
    acc = lax.dot_general(x_ref[...], w, NT_DIMS if transposed else (((1,), (0,)), ((), ())),
                          preferred_element_type=F32)
    if act == "relu2":
        acc = jnp.square(jnp.maximum(acc, 0.0))
    o_ref[...] = acc.astype(o_ref.dtype)


def matmul(x, w, layer, out_dtype, act=None, transposed=False, col0=0, n_cols=None, sides=(), bm=1024, bn=1024):
    M, K = x.shape
    if transposed:
        row_q = V7X_SUBLANES * (2 if w.dtype == BF16 else 1)
        assert col0 % row_q == 0 and (w.dtype == BF16) == (layer is None)
        N = n_cols
    else:
        assert col0 == 0 and n_cols is None
        N = w.shape[-1]
    bm = _tile(M, bm, V7X_SUBLANES)
    bn = _tile(N, bn, V7X_LANES)
    n_i = M // bm
    if transposed and layer is None:
        w_spec = pl.BlockSpec((pl.Element(bn), pl.Element(K)),
                              lambda j, i: (pl.multiple_of(col0 + j * bn, row_q), 0))
    elif transposed:
        w_spec = pl.BlockSpec((pl.Element(1), pl.Element(bn), pl.Element(K)),
                              lambda j, i: (layer, pl.multiple_of(col0 + j * bn, row_q), 0))
    elif layer is None:
        w_spec = pl.BlockSpec((K, bn), lambda j, i: (0, j))
    else:
        w_spec = pl.BlockSpec((None, K, bn), lambda j, i: (layer, 0, j))
    in_specs = [pl.BlockSpec((bm, K), lambda j, i: (i, 0)), w_spec]
    out_specs = [pl.BlockSpec((bm, bn), lambda j, i: (i, j))]
    out_shape = [jax.ShapeDtypeStruct((M, N), out_dtype)]
    args = [x, w]
    n_steps = (N // bn) * n_i
    for s_arr, s_layer in sides:
        _, s_rows, s_cols = s_arr.shape
        rb = next((r for r in range(16, s_rows + 1, 16) if s_rows % r == 0 and s_rows // r <= n_steps), s_rows)
        last = s_rows // rb - 1
        in_specs.append(pl.BlockSpec((None, rb, s_cols), functools.partial(
            lambda l, last, j, i: (l, jnp.minimum(j * n_i + i, last), 0), s_layer, last)))
        out_specs.append(pl.BlockSpec((rb, s_cols), functools.partial(
            lambda last, j, i: (jnp.minimum(j * n_i + i, last), 0), last)))
        out_shape.append(jax.ShapeDtypeStruct((s_rows, s_cols), BF16))
        args.append(s_arr)
    scratch = [pltpu.VMEM((bn, K) if transposed else (K, bn), BF16)] if w.dtype != BF16 else []
    outs = pl.pallas_call(
        functools.partial(_mm_kernel, act=act, transposed=transposed, n_side=len(sides)),
        grid=(N // bn, n_i),
        in_specs=in_specs,
        out_specs=out_specs,
        out_shape=out_shape,
        scratch_shapes=scratch,
        compiler_params=_params("arbitrary", "arbitrary"),
    )(*args)
    return outs[0] if not sides else outs


def _kv_proj_kernel(x_ref, w_ref, o2_ref, o5_ref, wb_ref, *, heads):
    @pl.when(pl.program_id(1) == 0)
    def _():
        wb_ref[...] = w_ref[...].astype(BF16)

    acc = jnp.dot(x_ref[...], wb_ref[...], preferred_element_type=F32)
    o2_ref[...] = acc
    hd = acc.shape[1] // heads
    for h in range(heads):
        o5_ref[0, :, h, :] = acc[:, h * hd:(h + 1) * hd]


def kv_proj(mem_b, w, nb, heads):
    M, K = mem_b.shape
    depth, _, N = w.shape
    mem_len = M // nb
    hd = N // heads
    return pl.pallas_call(
        functools.partial(_kv_proj_kernel, heads=heads),
        grid=(depth, nb),
        in_specs=[pl.BlockSpec((mem_len, K), lambda l, b: (b, 0)),
                  pl.BlockSpec((None, K, N), lambda l, b: (l, 0, 0), pipeline_mode=pl.Buffered(1))],
        out_specs=[pl.BlockSpec((None, mem_len, N), lambda l, b: (l, b, 0)),
                   pl.BlockSpec((None, 1, mem_len, heads, hd), lambda l, b: (l, b, 0, 0, 0))],
        out_shape=[jax.ShapeDtypeStruct((depth, M, N), F32),
                   jax.ShapeDtypeStruct((depth, nb, mem_len, heads, hd), F32)],
        scratch_shapes=[pltpu.VMEM((K, N), BF16)],
        compiler_params=_params("parallel", "arbitrary"),
    )(mem_b, w)


def _mm_ln_kernel(*refs, alpha, n_first, two_x, two_res):
    refs = list(refs)
    x_refs = [refs.pop(0) for _ in range(2 if two_x else 1)]
    w_ref = refs.pop(0)
    res_refs = [refs.pop(0) for _ in range(2 if two_res else 1)]
    g_ref, b_ref, of_ref, ob_ref = refs

    bm = of_ref.shape[0]
    n_chunks = 2 if bm >= 512 else 1
    ch = bm // n_chunks

    def body(src_ref, res_ref):
        for c in range(n_chunks):
            rows = slice(c * ch, (c + 1) * ch)
            m = jnp.dot(src_ref[rows, :], w_ref[...], preferred_element_type=F32)
            out = _layer_norm(alpha * res_ref[rows, :] + m, g_ref[...], b_ref[...])
            of_ref[rows, :] = out
            ob_ref[rows, :] = out.astype(BF16)

    if not (two_x or two_res):
        body(x_refs[0], res_refs[0])
    else:
        i = pl.program_id(0)

        @pl.when(i < n_first)
        def _():
            body(x_refs[0], res_refs[0])

        @pl.when(i >= n_first)
        def _():
            body(x_refs[-1], res_refs[-1])


def matmul_ln(x, w, layer, res, g, b, alpha, x2=None, res2=None):
    M1, K = x.shape
    M2 = 0 if x2 is None else x2.shape[0]
    M = M1 + M2
    N = w.shape[-1]
    assert res2 is None or (x2 is not None and res.shape[0] == M1 and res2.shape[0] == M2)
    n_x = 1 if x2 is None else 2
    n_res = 1 if res2 is None else 2
    w_bytes = K * N * 2
    per_row = n_x * 2 * (K * 2) + n_res * 2 * (N * 4) + 2 * (N * 4) + 2 * (N * 2) + 2 * (N * 4)
    bm = _tile(math.gcd(M1, M2) if M2 else M1, min(512, (VMEM_LIMIT - w_bytes) // per_row), 2 * V7X_SUBLANES)
    n_first = M1 // bm

    def row_specs(width, two):
        if not two:
            return [pl.BlockSpec((bm, width), lambda i: (i, 0))]
        return [pl.BlockSpec((bm, width), lambda i: (jnp.minimum(i, n_first - 1), 0)),
                pl.BlockSpec((bm, width), lambda i: (jnp.maximum(i - n_first, 0), 0))]

    if layer is None:
        w_spec = pl.BlockSpec((K, N), lambda i: (0, 0), pipeline_mode=pl.Buffered(1))
    else:
        w_spec = pl.BlockSpec((None, K, N), lambda i: (layer, 0, 0), pipeline_mode=pl.Buffered(1))
    return pl.pallas_call(
        functools.partial(_mm_ln_kernel, alpha=alpha, n_first=n_first, two_x=x2 is not None,
                          two_res=res2 is not None),
        grid=(M // bm,),
        in_specs=row_specs(K, x2 is not None) + [w_spec] + row_specs(N, res2 is not None) + [
            pl.BlockSpec((1, N), lambda i: (0, 0)),
            pl.BlockSpec((1, N), lambda i: (0, 0))],
        out_specs=[pl.BlockSpec((bm, N), lambda i: (i, 0)),
                   pl.BlockSpec((bm, N), lambda i: (i, 0))],
        out_shape=[jax.ShapeDtypeStruct((M, N), F32), jax.ShapeDtypeStruct((M, N), BF16)],
        compiler_params=_params("arbitrary"),
    )(*([x] if x2 is None else [x, x2]), w, *([res] if res2 is None else [res, res2]), g, b)


def _prompt_mixer_kernel(z_ref, xs_ref, bc_ref, gb_ref, gc_ref, hh_ref, dt_ref, convw_ref, convb_ref, dtb_ref,
                         alog_ref, dskip_ref, normg_ref, scw_ref, tril_ref, expand_ref,
                         y_ref, hout_ref, utail_ref, ctail_ref, state_ref, ext_ref, act_ref, extu_ref,
                         *, L, nc, G, R, P, N):
    RP = R * P
    D = G * RP
    GN = G * N
    CH = D + 2 * GN
    KC = convw_ref.shape[0]
    KS = scw_ref.shape[0]
    c = pl.program_id(1)

    @pl.when(c == 0)
    def _init():
        ext_ref[0:8, :] = jnp.zeros((8, CH), F32)
        extu_ref[0:8, :] = jnp.zeros((8, D), F32)
        state_ref[...] = jnp.zeros(state_ref.shape, F32)

    ext_ref[8:8 + L, 0:D] = xs_ref[...]
    ext_ref[8:8 + L, D:CH] = bc_ref[...]
    cw = min(512, CH)
    for j in range(CH // cw):
        cols = slice(j * cw, (j + 1) * cw)
        base = 8 - (KC - 1)
        acc = ext_ref[base:base + L, cols] * convw_ref[0:1, cols]
        for k in range(1, KC):
            acc = acc + ext_ref[base + k:base + k + L, cols] * convw_ref[k:k + 1, cols]
        act_ref[:, cols] = _silu(acc + convb_ref[:, cols])

    dt = _softplus(dt_ref[...] + dtb_ref[...])
    a = dt * (-jnp.exp(alog_ref[...]))
    acs = _sel_dot(tril_ref[...], a)
    dt_e = _dot_sel(dt, expand_ref[...])
    acs_e = _dot_sel(acs, expand_ref[...])
    acs_row = acs.T

    li = lax.broadcasted_iota(jnp.int32, (L, L), 0)
    si = lax.broadcasted_iota(jnp.int32, (L, L), 1)
    causal = si <= li
    ci = lax.broadcasted_iota(jnp.int32, (L, RP), 1)
    head_cols = [jnp.logical_and(ci >= r * P, ci < (r + 1) * P) for r in range(R)]

    for g in range(G):
        gc_ = slice(g * RP, (g + 1) * RP)
        xs_g = act_ref[:, gc_]
        b_g = act_ref[:, D + g * N:D + (g + 1) * N]
        c_g = act_ref[:, D + GN + g * N:D + GN + (g + 1) * N].astype(BF16)
        xdt = xs_g * dt_e[:, gc_]
        x_bd = jnp.concatenate([jnp.where(head_cols[r], xdt, 0.0) for r in range(R)], axis=0).astype(BF16)
        acs_g = acs_e[:, gc_]
        bt_g = b_g.T
        cb = jnp.dot(c_g, bt_g.astype(BF16), preferred_element_type=F32)

        ms, dbs = [], []
        for r in range(R):
            h = g * R + r
            col = acs[:, h:h + 1]
            row = acs_row[h:h + 1, :]
            ms.append(cb * jnp.exp(jnp.where(causal, col - row, -jnp.inf)))
            dbs.append(bt_g * jnp.exp(acs_row[h:h + 1, L - 1:L] - row))

        y = jnp.dot(jnp.concatenate(ms, axis=1).astype(BF16), x_bd, preferred_element_type=F32)
        h_prev = state_ref[g]
        y_off = jnp.dot(c_g, h_prev.astype(BF16), preferred_element_type=F32)
        dbt = jnp.concatenate(dbs, axis=1).astype(BF16)
        new_states = jnp.dot(dbt, x_bd, preferred_element_type=F32)
        state_ref[g] = h_prev * jnp.exp(acs_g[L - 1:L, :]) + new_states
        y = y + y_off * jnp.exp(acs_g)
        y = y + dskip_ref[:, gc_] * xs_g
        y = y * _silu(z_ref[:, gc_])
        y = y * lax.rsqrt(jnp.mean(y * y, -1, keepdims=True) + RMS_EPS)
        y_ref[:, gc_] = (y * normg_ref[:, gc_]).astype(BF16)

    cw = min(512, D)
    for j in range(D // cw):
        cols = slice(j * cw, (j + 1) * cw)
        extu_ref[8:8 + L, cols] = gc_ref[:, cols] * hh_ref[:, cols]
        base = 8 - (KS - 1)
        v = extu_ref[base:base + L, cols] * scw_ref[0:1, cols]
        for k in range(1, KS):
            v = v + extu_ref[base + k:base + k + L, cols] * scw_ref[k:k + 1, cols]
        y_ref[:, D + j * cw:D + (j + 1) * cw] = (gb_ref[:, cols] * v).astype(BF16)

    @pl.when(c == nc - 1)
    def _fin():
        utail_ref[0] = extu_ref[L:L + 8, :]
        ctail_ref[0] = ext_ref[L:L + 8, :]
        for g in range(G):
            hout_ref[0, g * R:(g + 1) * R] = state_ref[g].T.reshape(R, P, N)

    ext_ref[0:8, :] = ext_ref[L:L + 8, :]
    extu_ref[0:8, :] = extu_ref[L:L + 8, :]


def _mixer_consts(D, P):
    expand = (jnp.arange(V7X_LANES)[:, None] == (jnp.arange(D) // P)[None, :]).astype(BF16)
    return expand


def _proj_specs(rows, D, row_block):
    return [pl.BlockSpec((rows, D), functools.partial(lambda j, *idx: (row_block(*idx), j), j)) for j in (0, 1, 2)] * 2


def prompt_mixer(proj_a, proj_b, dt_raw, nb, T, cfg, wts):
    D, G, R, P, N = cfg["D"], cfg["G"], cfg["R"], cfg["P"], cfg["N"]
    H = G * R
    GN = G * N
    CH = D + 2 * GN
    L = SSD_CHUNK
    nc = T // L
    assert T % L == 0 and L == V7X_LANES and 2 * GN == D and R * P * G == D
    tril = jnp.tril(jnp.ones((L, L), BF16))
    expand = _mixer_consts(D, P)
    consts = list(wts) + [tril, expand]

    def row_block(b, c):
        return b * nc + c

    in_specs = (_proj_specs(L, D, row_block) + [pl.BlockSpec((L, V7X_LANES), lambda b, c: (b * nc + c, 0))]
                + [pl.BlockSpec(a.shape, functools.partial(lambda n, b, c: (0,) * n, a.ndim)) for a in consts])
    return pl.pallas_call(
        functools.partial(_prompt_mixer_kernel, L=L, nc=nc, G=G, R=R, P=P, N=N),
        grid=(nb, nc),
        in_specs=in_specs,
        out_specs=[pl.BlockSpec((L, 2 * D), lambda b, c: (b * nc + c, 0)),
                   pl.BlockSpec((1, H, P, N), lambda b, c: (b, 0, 0, 0)),
                   pl.BlockSpec((1, 8, D), lambda b, c: (b, 0, 0)),
                   pl.BlockSpec((1, 8, CH), lambda b, c: (b, 0, 0))],
        out_shape=[jax.ShapeDtypeStruct((nb * T, 2 * D), BF16),
                   jax.ShapeDtypeStruct((nb, H, P, N), F32),
                   jax.ShapeDtypeStruct((nb, 8, D), F32),
                   jax.ShapeDtypeStruct((nb, 8, CH), F32)],
        scratch_shapes=[pltpu.VMEM((G, N, R * P), F32), pltpu.VMEM((L + 8, CH), F32),
                        pltpu.VMEM((L, CH), F32), pltpu.VMEM((L + 8, D), F32)],
        compiler_params=_params("parallel", "arbitrary"),
    )(*([proj_a] * 3 + [proj_b] * 3 + [dt_raw] + consts))


def _sample_mixer_kernel(*refs, bb, T, G, R, P, N, n_prev):
    (z_ref, xs_ref, bc_ref, gb_ref, gc_ref, hh_ref, dt_ref, convw_ref, convb_ref, dtb_ref, alog_ref,
     dskip_ref, normg_ref, scw_ref, btril_ref, expand_ref, expand_t_ref, rep_ref,
     h0_ref, ctx_ref, scctx_ref) = refs[:21]
    prev_refs = refs[21:21 + n_prev]
    y_ref, hout_ref, utail_ref, ctail_ref, ext_ref, act_ref, extu_ref = refs[21 + n_prev:]
    H = G * R
    RP = R * P
    D = G * RP
    GN = G * N
    CH = D + 2 * GN
    HT = H * T
    KC = convw_ref.shape[0]
    KS = scw_ref.shape[0]
    rows = bb * T

    ext_ref[:, 8 - (KC - 1):8, :] = ctx_ref[...]
    ext_ref[:, 8:8 + T, 0:D] = xs_ref[...].reshape(bb, T, D)
    ext_ref[:, 8:8 + T, D:CH] = bc_ref[...].reshape(bb, T, 2 * GN)
    ctail_ref[...] = ext_ref[:, T:T + 8, :]
    cw = min(512, CH)
    for j in range(CH // cw):
        cols = slice(j * cw, (j + 1) * cw)
        base = 8 - (KC - 1)
        acc = ext_ref[:, base:base + T, cols] * convw_ref[0:1, cols]
        for k in range(1, KC):
            acc = acc + ext_ref[:, base + k:base + k + T, cols] * convw_ref[k:k + 1, cols]
        act_ref[:, cols] = _silu(acc + convb_ref[:, cols]).reshape(rows, cw)

    dt = _softplus(dt_ref[...] + dtb_ref[...])
    a = dt * (-jnp.exp(alog_ref[...]))
    acs = _sel_dot(btril_ref[...], a)
    dt_e = _dot_sel(dt, expand_ref[...])
    acs_e = _dot_sel(acs, expand_ref[...])
    acs_e3 = acs_e.reshape(bb, T, D)
    last_e = acs_e3[:, T - 1:T, :]
    xs_act = act_ref[:, 0:D]
    xdt = xs_act * dt_e
    xde = xdt * jnp.exp(last_e - acs_e3).reshape(rows, D)
    a1 = _dot_sel(acs, expand_t_ref[...]).reshape(bb, T, HT)
    l3 = lax.broadcasted_iota(jnp.int32, (bb, T, HT), 1)
    s3 = lax.broadcasted_iota(jnp.int32, (bb, T, HT), 2) & (T - 1)
    a2 = jnp.sum(jnp.where(l3 == s3, a1, 0.0), axis=1, keepdims=True)
    decay = jnp.exp(jnp.where(s3 <= l3, a1 - a2, -jnp.inf)).reshape(rows, HT)
    cd_heads = jnp.exp(acs)

    ri = lax.broadcasted_iota(jnp.int32, (HT, D), 0)
    ci = lax.broadcasted_iota(jnp.int32, (HT, D), 1)
    bd_mask = (ri >> (T.bit_length() - 1)) == (ci >> (P.bit_length() - 1))
    lane_g = lax.broadcasted_iota(jnp.int32, (T, HT), 1) >> ((R * T).bit_length() - 1)

    y_diag, y_off = [], []
    for b in range(bb):
        rs = slice(b * T, (b + 1) * T)
        b_pieces = [act_ref[rs, D + g * N:D + (g + 1) * N] for g in range(G)]
        c_st = jnp.concatenate([act_ref[rs, D + GN + g * N:D + GN + (g + 1) * N] for g in range(G)],
                               axis=0).astype(BF16)
        b_st = jnp.concatenate(b_pieces, axis=0).astype(BF16)
        call = lax.dot_general(c_st, b_st, NT_DIMS, preferred_element_type=F32)
        q = _dot_sel(call, rep_ref[...])
        cb = jnp.zeros((T, HT), F32)
        for g in range(G):
            cb = cb + jnp.where(lane_g == g, q[g * T:(g + 1) * T], 0.0)
        mcat = (cb * decay[rs]).astype(BF16)
        x_bd = jnp.where(bd_mask, jnp.concatenate([xdt[rs]] * H, axis=0), 0.0).astype(BF16)
        y_diag.append(jnp.dot(mcat, x_bd, preferred_element_type=F32))

        h0_b = h0_ref[b].reshape(D, N)
        yy = lax.dot_general(c_st, h0_b.astype(BF16), NT_DIMS, preferred_element_type=F32)
        y_off.append(jnp.concatenate([yy[g * T:(g + 1) * T, g * RP:(g + 1) * RP] for g in range(G)], axis=1))

        x_bd2 = jnp.where(bd_mask, jnp.concatenate([xde[rs]] * H, axis=0), 0.0).astype(BF16)
        b_rep = jnp.concatenate([b_pieces[h // R] for h in range(H)], axis=0).astype(BF16)
        new_states = lax.dot_general(x_bd2, b_rep, TN_DIMS, preferred_element_type=F32)
        r_last = b * T + T - 1
        cd = jnp.concatenate([jnp.broadcast_to(cd_heads[r_last:r_last + 1, h:h + 1], (P, N)) for h in range(H)],
                             axis=0)
        hout_ref[n_prev, b] = (h0_b * cd + new_states).reshape(H, P, N)
    for s, prev_ref in enumerate(prev_refs):
        hout_ref[s] = prev_ref[...]

    y = jnp.concatenate(y_diag, axis=0) + jnp.concatenate(y_off, axis=0) * jnp.exp(acs_e)
    y = y + dskip_ref[...] * xs_act
    y = y * _silu(z_ref[...])
    for g in range(G):
        gc_ = slice(g * RP, (g + 1) * RP)
        yg = y[:, gc_]
        yg = yg * lax.rsqrt(jnp.mean(yg * yg, -1, keepdims=True) + RMS_EPS)
        y_ref[:, gc_] = (yg * normg_ref[:, gc_]).astype(BF16)

    extu_ref[:, 8 - (KS - 1):8, :] = scctx_ref[...]
    cw = min(512, D)
    for j in range(D // cw):
        cols = slice(j * cw, (j + 1) * cw)
        extu_ref[:, 8:8 + T, cols] = (gc_ref[:, cols] * hh_ref[:, cols]).reshape(bb, T, cw)
        base = 8 - (KS - 1)
        v = extu_ref[:, base:base + T, cols] * scw_ref[0:1, cols]
        for k in range(1, KS):
            v = v + extu_ref[:, base + k:base + k + T, cols] * scw_ref[k:k + 1, cols]
        y_ref[:, D + j * cw:D + (j + 1) * cw] = (gb_ref[:, cols] * v.reshape(rows, cw)).astype(BF16)
    utail_ref[...] = extu_ref[:, T:T + 8, :]


def sample_mixer(proj_a, proj_b, dt_raw, row0, nb, T, cfg, wts, state_ssd, state_ssd_conv, state_short_conv,
                 layer_e, prev_states=(), bb=4):
    D, G, R, P, N = cfg["D"], cfg["G"], cfg["R"], cfg["P"], cfg["N"]
    H = G * R
    GN = G * N
    CH = D + 2 * GN
    bb = _tile(nb, bb, 2)
    rows = bb * T
    pow2 = lambda v: v & (v - 1) == 0
    assert T == V7X_SUBLANES and row0 % rows == 0 and 2 * GN == D and R * P * G == D
    assert pow2(P) and pow2(R)
    rb0 = row0 // rows
    kc1, ks1 = state_ssd_conv.shape[2], state_short_conv.shape[2]
    assert kc1 <= T and ks1 <= T
    seq = jnp.arange(rows) // T
    btril = jnp.logical_and(seq[:, None] == seq[None, :], jnp.arange(rows)[:, None] >= jnp.arange(rows)[None, :])
    expand = _mixer_consts(D, P)
    lane_h = jnp.arange(H * T) // T
    expand_t = (jnp.arange(V7X_LANES)[:, None] == lane_h[None, :]).astype(BF16)
    rep = jnp.logical_and((jnp.arange(G * T) // T)[:, None] == (lane_h // R)[None, :],
                          (jnp.arange(G * T) % T)[:, None] == (jnp.arange(H * T) % T)[None, :]).astype(BF16)
    consts = list(wts) + [btril.astype(BF16), expand, expand_t, rep]
    n_prev = len(prev_states)

    in_specs = (_proj_specs(rows, D, lambda i: rb0 + i) + [pl.BlockSpec((rows, V7X_LANES), lambda i: (rb0 + i, 0))]
                + [pl.BlockSpec(a.shape, functools.partial(lambda n, i: (0,) * n, a.ndim)) for a in consts]
                + [pl.BlockSpec((None, bb, H, P, N), lambda i: (layer_e, i, 0, 0, 0)),
                   pl.BlockSpec((None, bb, kc1, CH), lambda i: (layer_e, i, 0, 0)),
                   pl.BlockSpec((None, bb, ks1, D), lambda i: (layer_e, i, 0, 0))]
                + [pl.BlockSpec((bb, H, P, N), lambda i: (i, 0, 0, 0))] * n_prev)
    return pl.pallas_call(
        functools.partial(_sample_mixer_kernel, bb=bb, T=T, G=G, R=R, P=P, N=N, n_prev=n_prev),
        grid=(nb // bb,),
        in_specs=in_specs,
        out_specs=[pl.BlockSpec((rows, 2 * D), lambda i: (i, 0)),
                   pl.BlockSpec((n_prev + 1, bb, H, P, N), lambda i: (0, i, 0, 0, 0)),
                   pl.BlockSpec((bb, 8, D), lambda i: (i, 0, 0)),
                   pl.BlockSpec((bb, 8, CH), lambda i: (i, 0, 0))],
        out_shape=[jax.ShapeDtypeStruct((nb * T, 2 * D), BF16),
                   jax.ShapeDtypeStruct((n_prev + 1, nb, H, P, N), F32),
                   jax.ShapeDtypeStruct((nb, 8, D), F32),
                   jax.ShapeDtypeStruct((nb, 8, CH), F32)],
        scratch_shapes=[pltpu.VMEM((bb, 8 + T, CH), F32), pltpu.VMEM((rows, CH), F32),
                        pltpu.VMEM((bb, 8 + T, D), F32)],
        compiler_params=_params("parallel"),
    )(*([proj_a] * 3 + [proj_b] * 3 + [dt_raw] + consts + [state_ssd, state_ssd_conv, state_short_conv]
        + list(prev_states)))


def _pool_sample_kernel(st_ref, x_ref, p_ref, ext_ref, *, start_pos, dg):
    T = x_ref.shape[1]
    ext_ref[:, 16 - POOL_CTX:16, :] = st_ref[...]
    ext_ref[:, 16:16 + T, :] = x_ref[...]
    pos = start_pos + lax.broadcasted_iota(jnp.int32, (1, T, 1), 1)
    for gi, w in enumerate(POOL_WINDOWS):
        cols = slice(gi * dg, (gi + 1) * dg)
        s = ext_ref[:, 16:16 + T, cols]
        for j in range(1, w):
            s = s + ext_ref[:, 16 - j:16 - j + T, cols]
        cnt = jnp.minimum(pos + 1, w).astype(F32)
        p_ref[:, :, cols] = (s / cnt - x_ref[:, :, cols]).astype(BF16)


def pool_sample(state_pool, layer_o, x3, dg):
    B, T, D = x3.shape
    bb = _tile(B, 16, 1)
    return pl.pallas_call(
        functools.partial(_pool_sample_kernel, start_pos=PAST_LEN, dg=dg),
        grid=(B // bb,),
        in_specs=[pl.BlockSpec((None, bb, POOL_CTX, D), lambda i: (layer_o, i, 0, 0)),
                  pl.BlockSpec((bb, T, D), lambda i: (i, 0, 0))],
        out_specs=pl.BlockSpec((bb, T, D), lambda i: (i, 0, 0)),
        out_shape=jax.ShapeDtypeStruct((B, T, D), BF16),
        scratch_shapes=[pltpu.VMEM((bb, 16 + T, D), F32)],
        compiler_params=_params("parallel"),
    )(state_pool, x3)


POOL_TOP = 8 + 16


def _pool_ln_kernel(x_ref, halo_ref, ps_ref, w_ref, scale_ref, g_ref, b_ref, of_ref, ob_ref, ext_ref, tmp_ref, p_ref,
                    *, n_prompt_tiles, tiles_per_seq, dg, alpha):
    i = pl.program_id(0)
    bm = x_ref.shape[0]
    top = POOL_TOP
    end = top + bm

    @pl.when(i < n_prompt_tiles)
    def _prompt():
        first = (i % tiles_per_seq) == 0
        ext_ref[0:8, :] = jnp.zeros((8, ext_ref.shape[1]), F32)
        ext_ref[8:top, :] = jnp.where(first, 0.0, halo_ref[...])
        ext_ref[top:end, :] = x_ref[...]
        pos = (i % tiles_per_seq) * bm + lax.broadcasted_iota(jnp.int32, (bm, 1), 0)
        for gi, w in enumerate(POOL_WINDOWS):
            cols = slice(gi * dg, (gi + 1) * dg)
            src, src_cols, k, level = ext_ref, cols, 1, 0
            while 2 * k < w:
                dst = tmp_ref.at[level % 2]
                dst[0:8, :] = jnp.zeros((8, dg), F32)
                dst[8:end, :] = src[8:end, src_cols] + src[8 - k:end - k, src_cols]
                src, src_cols, k, level = dst, slice(None), 2 * k, level + 1
            s = src[top:end, src_cols] + src[top - k:end - k, src_cols]
            cnt = jnp.minimum(pos + 1, w).astype(F32)
            p_ref[:, cols] = (s / cnt - x_ref[:, cols]).astype(BF16)

    @pl.when(i >= n_prompt_tiles)
    def _sample():
        p_ref[...] = ps_ref[...]

    for gi in range(len(POOL_WINDOWS)):
        cols = slice(gi * dg, (gi + 1) * dg)
        m = jnp.dot(p_ref[:, cols], w_ref[gi], preferred_element_type=F32) * scale_ref[:, cols]
        ext_ref[top:end, cols] = alpha * x_ref[:, cols] + m
    out = _layer_norm(ext_ref[top:end, :], g_ref[...], b_ref[...])
    of_ref[...] = out
    ob_ref[...] = out.astype(BF16)


def pool_ln(x, p_sample, w_pool, scale, g, b, n_prompt_rows, T, alpha, bm=512):
    M, D = x.shape
    dg = D // len(POOL_WINDOWS)
    bm = _tile(math.gcd(math.gcd(n_prompt_rows, M - n_prompt_rows), T), bm, 16)
    npt = n_prompt_rows // bm
    hb = bm // 16
    return pl.pallas_call(
        functools.partial(_pool_ln_kernel, n_prompt_tiles=npt, tiles_per_seq=T // bm, dg=dg, alpha=alpha),
        grid=(M // bm,),
        in_specs=[pl.BlockSpec((bm, D), lambda i: (i, 0)),
                  pl.BlockSpec((16, D), lambda i: (jnp.maximum(i * hb - 1, 0), 0)),
                  pl.BlockSpec((bm, D), lambda i: (jnp.maximum(i - npt, 0), 0)),
                  pl.BlockSpec(w_pool.shape, lambda i: (0, 0, 0)),
                  pl.BlockSpec((1, D), lambda i: (0, 0)),
                  pl.BlockSpec((1, D), lambda i: (0, 0)),
                  pl.BlockSpec((1, D), lambda i: (0, 0))],
        out_specs=[pl.BlockSpec((bm, D), lambda i: (i, 0)), pl.BlockSpec((bm, D), lambda i: (i, 0))],
        out_shape=[jax.ShapeDtypeStruct((M, D), F32), jax.ShapeDtypeStruct((M, D), BF16)],
        scratch_shapes=[pltpu.VMEM((POOL_TOP + bm, D), F32), pltpu.VMEM((2, POOL_TOP + bm, dg), F32),
                        pltpu.VMEM((bm, D), BF16)],
        compiler_params=_params("arbitrary"),
    )(x, x, p_sample, w_pool, scale, g, b)


def _softmax_rows(s):
    m = jnp.max(s, -1, keepdims=True)
    e = jnp.exp(s - m)
    return e / jnp.sum(e, -1, keepdims=True)


def _attn_prompt_kernel(q_ref, k_ref, v_ref, o_ref, *, scale):
    s = lax.dot_general(q_ref[...], k_ref[...].astype(BF16), (((1,), (1,)), ((), ())),
                        preferred_element_type=F32) * scale
    p = _softmax_rows(s).astype(BF16)
    o_ref[...] = jnp.dot(p, v_ref[...].astype(BF16), preferred_element_type=F32).astype(BF16)


def attn_prompt(q, mk, mv, layer, nb, T, mem_len, heads, tq=2048):
    M, D = q.shape
    hd = D // heads
    tq = _tile(T, tq, V7X_SUBLANES)
    nt = T // tq
    return pl.pallas_call(
        functools.partial(_attn_prompt_kernel, scale=hd ** -0.5),
        grid=(nb, heads, nt),
        in_specs=[pl.BlockSpec((tq, hd), lambda b, h, t: (b * nt + t, h)),
                  pl.BlockSpec((None, mem_len, hd), lambda b, h, t: (layer, b, h)),
                  pl.BlockSpec((None, mem_len, hd), lambda b, h, t: (layer, b, h))],
        out_specs=pl.BlockSpec((tq, hd), lambda b, h, t: (b * nt + t, h)),
        out_shape=jax.ShapeDtypeStruct((M, D), BF16),
        input_output_aliases={0: 0},
        compiler_params=_params("parallel", "parallel", "parallel"),
    )(q, mk, mv)


def _attn_sample_kernel(q_ref, k_hbm, v_hbm, o_ref, kbuf, vbuf, sem, *, layer, scale, T):
    _, heads, bb, mem_len, hd = kbuf.shape
    i = pl.program_id(0)
    n = pl.num_programs(0)

    def copies(step, slot):
        cps = []
        for h in range(heads):
            for t, (src, dst) in enumerate(((k_hbm, kbuf), (v_hbm, vbuf))):
                cps.append(pltpu.make_async_copy(src.at[layer, pl.ds(step * bb, bb), :, h, :],
                                                 dst.at[slot, h], sem.at[slot, t, h]))
        return cps

    @pl.when(i == 0)
    def _():
        for cp in copies(0, 0):
            cp.start()

    slot = i % 2

    @pl.when(i + 1 < n)
    def _():
        for cp in copies(i + 1, 1 - slot):
            cp.start()

    for cp in copies(i, slot):
        cp.wait()

    slots = bb * heads
    q_all = q_ref[...].astype(F32)
    st = jnp.zeros((mem_len, slots * T), F32)
    for b in range(bb):
        for h in range(heads):
            s_ = b * heads + h
            q_bh = q_all[b * T:(b + 1) * T, h * hd:(h + 1) * hd]
            parts = []
            if s_:
                parts.append(jnp.zeros((s_ * T, hd), F32))
            parts.append(q_bh)
            if s_ + 1 < slots:
                parts.append(jnp.zeros(((slots - s_ - 1) * T, hd), F32))
            q_slot = jnp.concatenate(parts, axis=0).astype(BF16)
            st = st + lax.dot_general(kbuf[slot, h, b].astype(BF16), q_slot, (((1,), (1,)), ((), ())),
                                      preferred_element_type=F32)
    st = st * scale
    e = jnp.exp(st - jnp.max(st, 0, keepdims=True))
    p_all = (e / jnp.sum(e, 0, keepdims=True)).T
    rows = []
    for b in range(bb):
        outs = []
        for h in range(heads):
            s_ = b * heads + h
            p = p_all[s_ * T:(s_ + 1) * T, :].astype(BF16)
            outs.append(jnp.dot(p, vbuf[slot, h, b].astype(BF16), preferred_element_type=F32))
        rows.append(jnp.concatenate(outs, axis=1))
    o_ref[...] = jnp.concatenate(rows, axis=0).astype(BF16)


def attn_sample(q, cache_k, cache_v, layer, row0, T):
    _, B, mem_len, heads, hd = cache_k.shape
    M, D = q.shape
    bb = V7X_LANES // (heads * T)
    assert bb * heads * T == V7X_LANES and B % bb == 0
    rows = bb * T
    assert row0 % rows == 0 and rows % 16 == 0
    rb0 = row0 // rows
    return pl.pallas_call(
        functools.partial(_attn_sample_kernel, layer=layer, scale=hd ** -0.5, T=T),
        grid=(B // bb,),
        in_specs=[pl.BlockSpec((rows, D), lambda i: (rb0 + i, 0)),
                  pl.BlockSpec(memory_space=pl.ANY),
                  pl.BlockSpec(memory_space=pl.ANY)],
        out_specs=pl.BlockSpec((rows, D), lambda i: (rb0 + i, 0)),
        out_shape=jax.ShapeDtypeStruct((M, D), BF16),
        input_output_aliases={0: 0},
        scratch_shapes=[pltpu.VMEM((2, heads, bb, mem_len, hd), F32),
                        pltpu.VMEM((2, heads, bb, mem_len, hd), F32),
                        pltpu.SemaphoreType.DMA((2, 2, heads))],
        compiler_params=_params("arbitrary"),
    )(q, cache_k, cache_v)


def kernel(x_prompt, x_sample, state_ssd, state_ssd_conv, state_short_conv, state_pool, cache_mem_k, cache_mem_v, mem_prompt, w_in_even, ssd_conv_w, ssd_conv_b, ssd_dt_bias, ssd_a_log, ssd_d, ssd_norm_g, sc_conv_w, w_out_even, w_pool, pool_scale, wq_x, wk_x, wv_x, wo_x, w_up, w_down, ln_g, ln_b):
    BP, T, D = x_prompt.shape
    BS, TS, _ = x_sample.shape
    depth = wq_x.shape[0]
    n_even = w_in_even.shape[0]
    _, _, H, P, N = state_ssd.shape
    CH = state_ssd_conv.shape[-1]
    G = (CH - D) // (2 * N)
    R = H // G
    cfg = dict(D=D, G=G, R=R, P=P, N=N)
    mem_len, heads, hd = cache_mem_k.shape[2:]
    dg = w_pool.shape[-1]
    alpha = (2.0 * depth) ** 0.25
    MP, MS = BP * T, BS * TS
    o_dt = D + CH
    assert ssd_d.shape[-1] == H and H <= V7X_LANES and heads * hd == D

    xp2, xs2 = x_prompt.reshape(MP, D), x_sample.reshape(MS, D)
    xb = jnp.concatenate([xp2.astype(BF16), xs2.astype(BF16)], axis=0)
    xf = None
    mem_b = mem_prompt.reshape(BP * mem_len, D).astype(BF16)
    mk2, mk5 = kv_proj(mem_b, wk_x, BP, heads)
    mv2, mv5 = kv_proj(mem_b, wv_x, BP, heads)

    w_in_t = jnp.swapaxes(w_in_even, 1, 2)
    w_dt = jnp.pad(w_in_even[:, :, o_dt:o_dt + H], ((0, 0), (0, 0), (0, V7X_LANES - H))).astype(BF16)
    w_pool_b = w_pool.astype(BF16)

    def pad_lanes(v):
        return jnp.pad(v.astype(F32), (0, V7X_LANES - v.shape[0]))[None, :]

    def last_rows(x2, nb, t, n):
        return jnp.stack([x2[(b + 1) * t - n:(b + 1) * t] for b in range(nb)])

    kc, ks = ssd_conv_w.shape[1], sc_conv_w.shape[1]
    hs_p, hs_s, cbs_p, cbs_s, sbs_p, sbs_s, pbs_p, pbs_s = [], [], [], [], [], [], [], []
    h_s = None
    for layer in range(depth):
        ln = lambda j: (ln_g[layer, j][None, :], ln_b[layer, j][None, :])
        if layer % 2 == 0:
            e = layer // 2
            if layer == 0:
                proj_a, w_out_b = matmul(xb, w_in_t, e, F32, transposed=True, col0=0, n_cols=o_dt,
                                         sides=[(w_out_even, e)])
                proj_b, w_up_b, wq_b = matmul(xb, w_in_t, e, F32, transposed=True, col0=o_dt + H, n_cols=3 * D,
                                              sides=[(w_up, 0), (wq_x, 0)])
            else:
                proj_a, w_out_b = matmul(xb, w_in_b, None, F32, transposed=True, col0=0, n_cols=o_dt,
                                         sides=[(w_out_even, e)])
                proj_b = matmul(xb, w_in_b, None, F32, transposed=True, col0=o_dt + H, n_cols=3 * D)
            dt_raw = matmul(xb, w_dt, e, F32)
            wts = (ssd_conv_w[e], ssd_conv_b[e][None, :], pad_lanes(ssd_dt_bias[e]), pad_lanes(ssd_a_log[e]),
                   jnp.repeat(ssd_d[e].astype(F32), P)[None, :], ssd_norm_g[e][None, :], sc_conv_w[e])
            y_p, h_p, ut_p, ct_p = prompt_mixer(proj_a, proj_b, dt_raw, BP, T, cfg, wts)
            y_s, h_s, ut_s, ct_s = sample_mixer(proj_a, proj_b, dt_raw, MP, BS, TS, cfg, wts,
                                                state_ssd, state_ssd_conv, state_short_conv, e,
                                                prev_states=hs_s if e == n_even - 1 else ())
            hs_s.append(h_s[0])
            hs_p.append(h_p)
            cbs_p.append(ct_p[:, 8 - (kc - 1):])
            cbs_s.append(ct_s[:, 8 - (kc - 1):])
            sbs_p.append(ut_p[:, 8 - (ks - 1):])
            sbs_s.append(ut_s[:, 8 - (ks - 1):])
            if xf is None:
                xf, xb = matmul_ln(y_p, w_out_b, None, xp2, *ln(0), alpha, x2=y_s, res2=xs2)
            else:
                xf, xb = matmul_ln(y_p, w_out_b, None, xf, *ln(0), alpha, x2=y_s)
        else:
            o = layer // 2
            xs3 = xf[MP:].reshape(BS, TS, D)
            pbs_p.append(last_rows(xf, BP, T, POOL_CTX))
            pbs_s.append(jnp.concatenate([state_pool[o][:, TS:], xs3], axis=1))
            p_s = pool_sample(state_pool, o, xs3, dg).reshape(MS, D)
            xf, xb = pool_ln(xf, p_s, w_pool_b[o], pool_scale[o][None, :], *ln(0), MP, T, alpha)

        q, wo_b = matmul(xb, wq_b, None, BF16, sides=[(wo_x, layer)])
        att = attn_prompt(q, mk2, mv2, layer, BP, T, mem_len, heads)
        att = attn_sample(att, cache_mem_k, cache_mem_v, layer, MP, TS)
        xf, xb = matmul_ln(att, wo_b, None, xf, *ln(1), alpha)

        nxt = layer + 1
        sides = [(w_down, layer)]
        if nxt < depth:
            sides += [(w_up, nxt), (wq_x, nxt)] + ([(w_in_t, nxt // 2)] if nxt % 2 == 0 else [])
        outs = matmul(xb, w_up_b, None, BF16, act="relu2", sides=sides)
        hid, w_down_b = outs[0], outs[1]
        if nxt < depth:
            w_up_b, wq_b = outs[2], outs[3]
            w_in_b = outs[4] if nxt % 2 == 0 else None
        xf, xb = matmul_ln(hid, w_down_b, None, xf, *ln(2), alpha)

    return (xf[:MP].reshape(BP, T, D), xf[MP:].reshape(BS, TS, D),
            jnp.stack(hs_p), h_s, jnp.stack(cbs_p), jnp.stack(cbs_s),
            jnp.stack(sbs_p), jnp.stack(sbs_s), jnp.stack(pbs_p), jnp.stack(pbs_s), mk5, mv5)
```
